```python
import math
import jax, jax.numpy as jnp
from jax import lax
import numpy as np

D_MODEL = 4096
BATCH = 2
SEQ = 8192
DEPTH = 2
DEC_BATCH = 4
DEC_SEQ = 2048
PAST_LEN = 128

HEAD_DIM = 128
GRID_W = 64
NA_HEADS = D_MODEL // 2 // HEAD_DIM
NA_WIDTH = NA_HEADS * HEAD_DIM
NA_ROWS = 8
NA_COLS = 16
FNET_GROUPS = 4
FNET_GROUP_DIM = D_MODEL // 2 // FNET_GROUPS
FNET_WIDTH = FNET_GROUPS * FNET_GROUP_DIM
EVEN_IN = 3 * NA_WIDTH + FNET_WIDTH
DIFF_HEADS = D_MODEL // (2 * HEAD_DIM)
DIFF_WIDTH = DIFF_HEADS * 2 * HEAD_DIM
ODD_IN = 3 * DIFF_WIDTH
Q_BLOCK = 128
T5_BUCKETS = 32
T5_MAX_DIST = 128
FFN_HIDDEN = ((8 * D_MODEL + 3 * 256 - 1) // (3 * 256)) * 256
N_EVEN = (DEPTH + 1) // 2
N_ODD = DEPTH // 2
RMS_EPS = 1e-6
SUBLN_EPS = 1e-5

kernel_name = 'hybrid_natten_fnet_diffattn_encoder'


def rmsnorm(x, g, eps=RMS_EPS):
    xf = x.astype(jnp.float32)
    y = xf * lax.rsqrt(jnp.mean(xf * xf, axis=-1, keepdims=True) + eps)
    return (y * g.astype(jnp.float32)).astype(x.dtype)


def lambda_init_fn(layer):
    return 0.8 - 0.6 * math.exp(-0.3 * layer)


def t5_bucket(rel):
    nb = T5_BUCKETS // 2
    max_exact = nb // 2
    ret = (rel > 0).astype(jnp.int32) * nb
    n = jnp.abs(rel)
    nf = jnp.maximum(n, 1).astype(jnp.float32)
    large = max_exact + (jnp.log(nf / max_exact) / math.log(T5_MAX_DIST / max_exact)
                         * (nb - max_exact)).astype(jnp.int32)
    large = jnp.minimum(large, nb - 1)
    return ret + jnp.where(n < max_exact, n, large)


def neighbourhood_attention(q, k, v, rpb):
    B, S, H, d = q.shape
    rows = S // GRID_W
    kh = min(NA_ROWS, rows)
    kw = NA_COLS
    qg = q.reshape(B, rows, GRID_W, H, d).transpose(1, 0, 2, 3, 4)
    kg = k.reshape(B, rows, GRID_W, H, d)
    vg = v.reshape(B, rows, GRID_W, H, d)
    row_ids = jnp.arange(rows, dtype=jnp.int32)
    row_start = jnp.clip(row_ids - kh // 2, 0, rows - kh)
    col_ids = jnp.arange(GRID_W, dtype=jnp.int32)
    col_start = jnp.clip(col_ids - kw // 2, 0, GRID_W - kw)
    col_idx = col_start[:, None] + jnp.arange(kw, dtype=jnp.int32)[None, :]
    col_bias_idx = col_idx - col_ids[:, None] + (NA_COLS - 1)
    scale = HEAD_DIM ** -0.5

    def row_block(args):
        q_row, r, rs = args
        k_rows = lax.dynamic_slice_in_dim(kg, rs, kh, axis=1)
        v_rows = lax.dynamic_slice_in_dim(vg, rs, kh, axis=1)
        k_win = k_rows[:, :, col_idx]
        v_win = v_rows[:, :, col_idx]
        s = jnp.einsum('bchd,bicjhd->bhcij', q_row, k_win,
                       preferred_element_type=jnp.float32) * scale
        dr = rs + jnp.arange(kh, dtype=jnp.int32) - r + (NA_ROWS - 1)
        bias = rpb[:, dr][:, :, col_bias_idx]
        s = s + jnp.transpose(bias, (0, 2, 1, 3)).astype(jnp.float32)[None]
        p = jax.nn.softmax(s.reshape(B, H, GRID_W, kh * kw), axis=-1)
        p = p.reshape(B, H, GRID_W, kh, kw).astype(v.dtype)
        return jnp.einsum('bhcij,bicjhe->bche', p, v_win)

    out = lax.map(row_block, (qg, row_ids, row_start))
    return out.transpose(1, 0, 2, 3, 4).reshape(B, S, H, d)


def fourier_mix(u):
    B, S, _ = u.shape
    ug = u.reshape(B, S, FNET_GROUPS, FNET_GROUP_DIM).astype(jnp.float32)
    f = jnp.fft.fft2(ug, axes=(1, 3), norm='ortho')
    return jnp.real(f).astype(u.dtype).reshape(B, S, FNET_WIDTH)


def differential_attention(q1, q2, k1, k2, v, t5_bias, lam):
    B, S, H, d = q1.shape
    nblk = S // Q_BLOCK
    qb1 = q1.reshape(B, nblk, Q_BLOCK, H, d).transpose(1, 0, 2, 3, 4)
    qb2 = q2.reshape(B, nblk, Q_BLOCK, H, d).transpose(1, 0, 2, 3, 4)
    starts = jnp.arange(nblk, dtype=jnp.int32) * Q_BLOCK
    kpos = jnp.arange(S, dtype=jnp.int32)
    scale = HEAD_DIM ** -0.5

    def block(args):
        qa, qb, start = args
        qpos = start + jnp.arange(Q_BLOCK, dtype=jnp.int32)
        bias = t5_bias[t5_bucket(kpos[None, :] - qpos[:, None])]
        bias = jnp.transpose(bias, (2, 0, 1)).astype(jnp.float32)[None]
        s1 = jnp.einsum('bqhd,bkhd->bhqk', qa, k1, preferred_element_type=jnp.float32) * scale + bias
        s2 = jnp.einsum('bqhd,bkhd->bhqk', qb, k2, preferred_element_type=jnp.float32) * scale + bias
        p = jax.nn.softmax(s1, axis=-1) - lam * jax.nn.softmax(s2, axis=-1)
        return jnp.einsum('bhqk,bkhe->bqhe', p.astype(v.dtype), v)

    out = lax.map(block, (qb1, qb2, starts))
    return out.transpose(1, 0, 2, 3, 4).reshape(B, S, H, 2 * d)


def trunk(x, norm_mix, norm_ffn, w_in_even, rpb_na, w_out_even, w_in_odd,
          lambda_q1, lambda_k1, lambda_q2, lambda_k2, subln_w, w_out_odd, t5_bias,
          w_gate, w_up, w_down, norm_final):
    B, S, _ = x.shape
    for layer in range(DEPTH):
        h = rmsnorm(x, norm_mix[layer])
        if layer % 2 == 0:
            e = layer // 2
            proj = h @ w_in_even[e]
            qa = proj[..., :NA_WIDTH].reshape(B, S, NA_HEADS, HEAD_DIM)
            ka = proj[..., NA_WIDTH:2 * NA_WIDTH].reshape(B, S, NA_HEADS, HEAD_DIM)
            va = proj[..., 2 * NA_WIDTH:3 * NA_WIDTH].reshape(B, S, NA_HEADS, HEAD_DIM)
            ub = proj[..., 3 * NA_WIDTH:]
            oa = neighbourhood_attention(qa, ka, va, rpb_na[e]).reshape(B, S, NA_WIDTH)
            ob = fourier_mix(ub)
            x = x + jnp.concatenate([oa, ob], axis=-1) @ w_out_even[e]
        else:
            o = layer // 2
            proj = h @ w_in_odd[o]
            q = proj[..., :DIFF_WIDTH].reshape(B, S, DIFF_HEADS, 2, HEAD_DIM)
            k = proj[..., DIFF_WIDTH:2 * DIFF_WIDTH].reshape(B, S, DIFF_HEADS, 2, HEAD_DIM)
            v = proj[..., 2 * DIFF_WIDTH:].reshape(B, S, DIFF_HEADS, 2 * HEAD_DIM)
            lam_init = lambda_init_fn(layer)
            lam = (jnp.exp(jnp.sum(lambda_q1[o].astype(jnp.float32) * lambda_k1[o].astype(jnp.float32)))
                   - jnp.exp(jnp.sum(lambda_q2[o].astype(jnp.float32) * lambda_k2[o].astype(jnp.float32)))
                   + lam_init)
            att = differential_attention(q[..., 0, :], q[..., 1, :], k[..., 0, :], k[..., 1, :],
                                         v, t5_bias, lam)
            att = rmsnorm(att, subln_w[o], eps=SUBLN_EPS) * (1.0 - lam_init)
            x = x + att.reshape(B, S, DIFF_WIDTH) @ w_out_odd[o]
        h = rmsnorm(x, norm_ffn[layer])
        x = x + (jax.nn.silu(h @ w_gate[layer]) * (h @ w_up[layer])) @ w_down[layer]
    return rmsnorm(x, norm_final)


def setup_inputs(seed: int = 0) -> dict:
    key = jax.random.key(seed)
    ks = jax.random.split(key, 20)
    f32 = jnp.float32

    def nrm(k, shape, scale):
        return jax.random.normal(k, shape, f32) * scale

    return {
        'x_prompt': nrm(ks[0], (BATCH, SEQ, D_MODEL), 1.0),
        'x_sample': nrm(ks[1], (DEC_BATCH, DEC_SEQ, D_MODEL), 1.0),
        'norm_mix': 1.0 + nrm(ks[2], (DEPTH, D_MODEL), 0.01),
        'norm_ffn': 1.0 + nrm(ks[3], (DEPTH, D_MODEL), 0.01),
        'w_in_even': nrm(ks[4], (N_EVEN, D_MODEL, EVEN_IN), D_MODEL ** -0.5),
        'rpb_na': nrm(ks[5], (N_EVEN, NA_HEADS, 2 * NA_ROWS - 1, 2 * NA_COLS - 1), 0.1),
        'w_out_even': nrm(ks[6], (N_EVEN, NA_WIDTH + FNET_WIDTH, D_MODEL), (NA_WIDTH + FNET_WIDTH) ** -0.5),
        'w_in_odd': nrm(ks[7], (N_ODD, D_MODEL, ODD_IN), D_MODEL ** -0.5),
        'lambda_q1': nrm(ks[8], (N_ODD, HEAD_DIM), 0.1),
        'lambda_k1': nrm(ks[9], (N_ODD, HEAD_DIM), 0.1),
        'lambda_q2': nrm(ks[10], (N_ODD, HEAD_DIM), 0.1),
        'lambda_k2': nrm(ks[11], (N_ODD, HEAD_DIM), 0.1),
        'subln_w': 1.0 + nrm(ks[12], (N_ODD, 2 * HEAD_DIM), 0.01),
        'w_out_odd': nrm(ks[13], (N_ODD, DIFF_WIDTH, D_MODEL), DIFF_WIDTH ** -0.5),
        't5_bias': nrm(ks[14], (T5_BUCKETS, DIFF_HEADS), 0.1),
        'w_gate': nrm(ks[15], (DEPTH, D_MODEL, FFN_HIDDEN), D_MODEL ** -0.5),
        'w_up': nrm(ks[16], (DEPTH, D_MODEL, FFN_HIDDEN), D_MODEL ** -0.5),
        'w_down': nrm(ks[17], (DEPTH, FFN_HIDDEN, D_MODEL), FFN_HIDDEN ** -0.5),
        'norm_final': 1.0 + nrm(ks[18], (D_MODEL,), 0.01),
    }


def reference(x_prompt, x_sample, norm_mix, norm_ffn, w_in_even, rpb_na, w_out_even, w_in_odd,
              lambda_q1, lambda_k1, lambda_q2, lambda_k2, subln_w, w_out_odd, t5_bias,
              w_gate, w_up, w_down, norm_final):
    y_prompt = trunk(x_prompt, norm_mix, norm_ffn, w_in_even, rpb_na, w_out_even, w_in_odd,
                     lambda_q1, lambda_k1, lambda_q2, lambda_k2, subln_w, w_out_odd, t5_bias,
                     w_gate, w_up, w_down, norm_final)
    y_sample = trunk(x_sample, norm_mix, norm_ffn, w_in_even, rpb_na, w_out_even, w_in_odd,
                     lambda_q1, lambda_k1, lambda_q2, lambda_k2, subln_w, w_out_odd, t5_bias,
                     w_gate, w_up, w_down, norm_final)
    return (y_prompt, y_sample)
```

```python
import functools
import math

import jax
import jax.numpy as jnp
from jax import lax
from jax.experimental import pallas as pl
from jax.experimental.pallas import tpu as pltpu

F32 = jnp.float32
BF16 = jnp.bfloat16

HEAD_DIM = 128
GRID_W = 64
NA_ROWS = 8
NA_COLS = 16
FNET_GROUPS = 4
T5_BUCKETS = 32
T5_MAX_DIST = 128
RMS_EPS = 1e-6
SUBLN_EPS = 1e-5
NEG_BIG = -1e30

V7X_VMEM_LIMIT = 56 * 1024 * 1024
ATT_TILE = 512
FFN_PAD = 1024


def _params(sem, vmem_bytes):
    return pltpu.CompilerParams(dimension_semantics=sem,
                                vmem_limit_bytes=int(min(V7X_VMEM_LIMIT, vmem_bytes * 3 // 2)))


def _rmsnorm_kernel(x_ref, g_ref, o_ref, *, eps):
    x = x_ref[...]
    inv = lax.rsqrt(jnp.mean(x * x, axis=-1, keepdims=True) + eps)
    o_ref[...] = (x * inv * g_ref[...]).astype(o_ref.dtype)


def rmsnorm(x, g, out_dtype, *, row_start=0, rows=None, bm=256):
    total, d = x.shape
    rows = total if rows is None else rows
    off = row_start // bm
    assert row_start % bm == 0 and rows % bm == 0
    return pl.pallas_call(
        functools.partial(_rmsnorm_kernel, eps=RMS_EPS),
        out_shape=jax.ShapeDtypeStruct((rows, d), out_dtype),
        grid=(rows // bm,),
        in_specs=[pl.BlockSpec((bm, d), lambda i: (i + off, 0)),
                  pl.BlockSpec((1, d), lambda i: (0, 0))],
        out_specs=pl.BlockSpec((bm, d), lambda i: (i, 0)),
        compiler_params=_params(("parallel",), 6 * bm * d * 4),
        name="rmsnorm",
    )(x, g.reshape(1, d).astype(F32))


def _matmul_kernel(*refs, nk, has_res):
    if has_res:
        a_ref, w_ref, r_ref, o_ref = refs[:4]
    else:
        a_ref, w_ref, o_ref = refs[:3]
        r_ref = None
    part = jnp.dot(a_ref[...], w_ref[...], preferred_element_type=F32)
    if nk == 1:
        if r_ref is not None:
            part = part + r_ref[...]
        o_ref[...] = part.astype(o_ref.dtype)
        return
    k = pl.program_id(2)

    @pl.when(k == 0)
    def _():
        o_ref[...] = part + r_ref[...] if r_ref is not None else part

    @pl.when(k > 0)
    def _():
        o_ref[...] += part


def matmul(a, w, out_dtype, *, residual=None, bm=1024, bn=1024, tk=None):
    m, kdim = a.shape
    _, n = w.shape
    tk = kdim if tk is None else tk
    bm, bn = min(bm, m), min(bn, n)
    assert m % bm == 0 and n % bn == 0 and kdim % tk == 0
    nk = kdim // tk
    has_res = residual is not None
    in_specs = [pl.BlockSpec((bm, tk), lambda i, j, k: (i, k)),
                pl.BlockSpec((tk, bn), lambda i, j, k: (k, j))]
    args = [a, w]
    if has_res:
        in_specs.append(pl.BlockSpec((bm, bn), lambda i, j, k: (i, j)))
        args.append(residual)
    assert nk == 1 or out_dtype == F32
    out_bytes = jnp.dtype(out_dtype).itemsize
    vmem = (2 * (bm * tk + tk * bn) * 2 + 2 * bm * bn * out_bytes + (2 * bm * bn * 4 if has_res else 0)
            + bm * bn * 4)
    return pl.pallas_call(
        functools.partial(_matmul_kernel, nk=nk, has_res=has_res),
        out_shape=jax.ShapeDtypeStruct((m, n), out_dtype),
        grid=(m // bm, n // bn, nk),
        in_specs=in_specs,
        out_specs=pl.BlockSpec((bm, bn), lambda i, j, k: (i, j)),
        compiler_params=_params(("parallel", "parallel", "arbitrary"), vmem),
        name="matmul",
    )(*args)


def _gate_up_kernel(a_ref, wg_ref, wu_ref, o_ref):
    a = a_ref[...]
    g = jnp.dot(a, wg_ref[...], preferred_element_type=F32)
    u = jnp.dot(a, wu_ref[...], preferred_element_type=F32)
    o_ref[...] = (g * jax.nn.sigmoid(g) * u).astype(o_ref.dtype)


def gate_up(h, wg, wu, *, bm=1024, bn=512):
    m, kdim = h.shape
    _, n = wg.shape
    assert m % bm == 0 and n % bn == 0
    vmem = 2 * bm * kdim * 2 + 4 * kdim * bn * 2 + 2 * bm * bn * 2 + 4 * bm * bn * 4
    return pl.pallas_call(
        _gate_up_kernel,
        out_shape=jax.ShapeDtypeStruct((m, n), BF16),
        grid=(m // bm, n // bn),
        in_specs=[pl.BlockSpec((bm, kdim), lambda i, j: (i, 0)),
                  pl.BlockSpec((kdim, bn), lambda i, j: (0, j)),
                  pl.BlockSpec((kdim, bn), lambda i, j: (0, j))],
        out_specs=pl.BlockSpec((bm, bn), lambda i, j: (i, j)),
        compiler_params=_params(("parallel", "parallel"), vmem),
        name="gate_up",
    )(h, wg, wu)


def _na_kernel(q_ref, k_ref, v_ref, bias_ref, o_ref, *, rows, kh):
    scale = HEAD_DIM ** -0.5
    win = kh * GRID_W

    def row(r, carry):
        rs = jnp.clip(r - kh // 2, 0, rows - kh)
        q = q_ref[pl.ds(pl.multiple_of(r * GRID_W, GRID_W), GRID_W), :]
        koff = pl.multiple_of(rs * GRID_W, GRID_W)
        kw = k_ref[pl.ds(koff, win), :]
        vw = v_ref[pl.ds(koff, win), :]
        s = lax.dot_general(q, kw, (((1,), (1,)), ((), ())), preferred_element_type=F32)
        s = s * scale + bias_ref[r - rs]
        p = jnp.exp(s - jnp.max(s, axis=-1, keepdims=True))
        l = jnp.sum(p, axis=-1, keepdims=True)
        o = jnp.dot(p.astype(vw.dtype), vw, preferred_element_type=F32) / l
        o_ref[pl.ds(pl.multiple_of(r * GRID_W, GRID_W), GRID_W), :] = o.astype(o_ref.dtype)
        return carry

    lax.fori_loop(0, rows, row, 0)


def _na_bias_table(rpb, kh):
    heads = rpb.shape[0]
    c = jnp.arange(GRID_W, dtype=jnp.int32)
    cs = jnp.clip(c - NA_COLS // 2, 0, GRID_W - NA_COLS)
    j = jnp.arange(GRID_W, dtype=jnp.int32)
    valid = (j[None, :] >= cs[:, None]) & (j[None, :] < cs[:, None] + NA_COLS)
    dc = jnp.clip(j[None, :] - c[:, None] + (NA_COLS - 1), 0, 2 * NA_COLS - 2)
    delta = jnp.arange(kh, dtype=jnp.int32)
    i = jnp.arange(kh, dtype=jnp.int32)
    dr = i[None, :] - delta[:, None] + (NA_ROWS - 1)
    tab = rpb.astype(F32)[:, dr[:, None, :, None], dc[None, :, None, :]]
    tab = jnp.where(valid[None, None, :, None, :], tab, NEG_BIG)
    return tab.reshape(heads, kh, GRID_W, kh * GRID_W)


def neighbourhood_attention(qkv, bias_tab, *, row_start, batch, seq, heads):
    rows = seq // GRID_W
    kh = bias_tab.shape[1]
    off = row_start // seq
    assert row_start % seq == 0
    vmem = 2 * 4 * seq * HEAD_DIM * 2 + 2 * bias_tab[0].size * 4 + (8 << 20)
    return pl.pallas_call(
        functools.partial(_na_kernel, rows=rows, kh=kh),
        out_shape=jax.ShapeDtypeStruct((batch * seq, heads * HEAD_DIM), BF16),
        grid=(batch, heads),
        in_specs=[pl.BlockSpec((seq, HEAD_DIM), lambda b, h: (b + off, h)),
                  pl.BlockSpec((seq, HEAD_DIM), lambda b, h: (b + off, heads + h)),
                  pl.BlockSpec((seq, HEAD_DIM), lambda b, h: (b + off, 2 * heads + h)),
                  pl.BlockSpec((None, kh, GRID_W, kh * GRID_W), lambda b, h: (h, 0, 0, 0))],
        out_specs=pl.BlockSpec((seq, HEAD_DIM), lambda b, h: (b, h)),
        compiler_params=_params(("parallel", "parallel"), vmem),
        name="neighbourhood_attention",
    )(qkv, qkv, qkv, bias_tab)


def _dft_tables(n):
    idx = jnp.arange(n, dtype=jnp.int32)
    prod = (idx[:, None] * idx[None, :]) % n
    ang = prod.astype(F32) * (2.0 * math.pi / n)
    norm = n ** -0.5
    return (jnp.cos(ang) * norm).astype(BF16), (jnp.sin(ang) * norm).astype(BF16)


def _seq_dft_kernel(c_ref, s_ref, y1_ref, y2_ref, o_ref, acc_ref, *, nk):
    k = pl.program_id(2)
    part = (jnp.dot(c_ref[...], y1_ref[...], preferred_element_type=F32)
            - jnp.dot(s_ref[...], y2_ref[...], preferred_element_type=F32))

    @pl.when(k == 0)
    def _():
        acc_ref[...] = part

    @pl.when(k > 0)
    def _():
        acc_ref[...] += part

    @pl.when(k == nk - 1)
    def _():
        o_ref[...] = acc_ref[...].astype(o_ref.dtype)


def fourier_mix(u, *, row_start, batch, seq, group_dim, bm=1024, tk=1024):
    total, width = u.shape
    groups = width // group_dim
    cg, sg = _dft_tables(group_dim)
    ug = u.reshape(total * groups, group_dim)
    y1 = matmul(ug, cg, BF16, bm=2048, bn=group_dim).reshape(total, width)
    y2 = matmul(ug, sg, BF16, bm=2048, bn=group_dim).reshape(total, width)
    cs, ss = _dft_tables(seq)
    bm, tk = min(bm, seq), min(tk, seq)
    nk = seq // tk
    off = row_start // tk
    assert row_start % tk == 0
    vmem = 2 * 2 * bm * tk * 2 + 2 * 2 * tk * width * 2 + 2 * bm * width * 2 + 3 * bm * width * 4
    return pl.pallas_call(
        functools.partial(_seq_dft_kernel, nk=nk),
        out_shape=jax.ShapeDtypeStruct((batch * seq, width), BF16),
        grid=(batch, seq // bm, nk),
        in_specs=[pl.BlockSpec((bm, tk), lambda b, i, k: (i, k)),
                  pl.BlockSpec((bm, tk), lambda b, i, k: (i, k)),
                  pl.BlockSpec((tk, width), lambda b, i, k: (off + b * nk + k, 0)),
                  pl.BlockSpec((tk, width), lambda b, i, k: (off + b * nk + k, 0))],
        out_specs=pl.BlockSpec((bm, width), lambda b, i, k: (b * (seq // bm) + i, 0)),
        scratch_shapes=[pltpu.VMEM((bm, width), F32)],
        compiler_params=_params(("parallel", "parallel", "arbitrary"), vmem),
        name="seq_dft",
    )(cs, ss, y1, y2)


def _t5_bucket(rel):
    nb = T5_BUCKETS // 2
    max_exact = nb // 2
    ret = (rel > 0).astype(jnp.int32) * nb
    n = jnp.abs(rel)
    nf = jnp.maximum(n, 1).astype(F32)
    large = max_exact + (jnp.log(nf / max_exact) / math.log(T5_MAX_DIST / max_exact)
                         * (nb - max_exact)).astype(jnp.int32)
    large = jnp.minimum(large, nb - 1)
    return ret + jnp.where(n < max_exact, n, large)


def _t5_near_bias(t5_bias, t):
    assert t >= T5_MAX_DIST
    qi = jnp.arange(t, dtype=jnp.int32)[None, :, None]
    ki = jnp.arange(t, dtype=jnp.int32)[None, None, :]
    d = jnp.arange(-1, 2, dtype=jnp.int32)[:, None, None]
    bucket = _t5_bucket(ki + d * t - qi)
    near = jnp.transpose(t5_bias.astype(F32)[bucket], (3, 0, 1, 2))
    far = jnp.stack([t5_bias[T5_BUCKETS // 2 - 1], t5_bias[T5_BUCKETS - 1]]).astype(F32)
    return near, far


def _diff_attn_kernel(far_ref, q_ref, k_ref, v_ref, nb_ref, lq1_ref, lk1_ref, lq2_ref, lk2_ref, w_ref, o_ref,
                      m1_ref, l1_ref, a1_ref, m2_ref, l2_ref, a2_ref, *, nkv, t, lam_init):
    h = pl.program_id(1)
    i = pl.program_id(2)
    scale = HEAD_DIM ** -0.5
    d = HEAD_DIM
    for m_ref, l_ref, a_ref in ((m1_ref, l1_ref, a1_ref), (m2_ref, l2_ref, a2_ref)):
        m_ref[...] = jnp.full(m_ref.shape, NEG_BIG, F32)
        l_ref[...] = jnp.zeros(l_ref.shape, F32)
        a_ref[...] = jnp.zeros(a_ref.shape, F32)

    def step(j, bias):
        koff = pl.multiple_of(j * t, t)
        kj = k_ref[pl.ds(koff, t), :]
        vj = v_ref[pl.ds(koff, t), :]
        for half, (m_ref, l_ref, a_ref) in enumerate(((m1_ref, l1_ref, a1_ref), (m2_ref, l2_ref, a2_ref))):
            qh = q_ref[:, half * d:(half + 1) * d]
            kh = kj[:, half * d:(half + 1) * d]
            s = lax.dot_general(qh, kh, (((1,), (1,)), ((), ())), preferred_element_type=F32) * scale + bias
            m_old = m_ref[...]
            m_new = jnp.maximum(m_old, jnp.max(s, axis=-1, keepdims=True))
            alpha = jnp.exp(m_old - m_new)
            p = jnp.exp(s - m_new)
            l_ref[...] = alpha * l_ref[...] + jnp.sum(p, axis=-1, keepdims=True)
            a_ref[...] = alpha * a_ref[...] + jnp.dot(p.astype(vj.dtype), vj, preferred_element_type=F32)
            m_ref[...] = m_new

    def far_left(j, c):
        step(j, far_ref[0, h])
        return c

    def far_right(j, c):
        step(j, far_ref[1, h])
        return c

    lax.fori_loop(0, jnp.maximum(i - 1, 0), far_left, 0)
    for dd in (-1, 0, 1):
        j = i + dd

        @pl.when((j >= 0) & (j < nkv))
        def _():
            step(j, nb_ref[dd + 1])

    lax.fori_loop(jnp.minimum(i + 2, nkv), nkv, far_right, 0)

    lam = (jnp.exp(jnp.sum(lq1_ref[...] * lk1_ref[...], axis=-1, keepdims=True))
           - jnp.exp(jnp.sum(lq2_ref[...] * lk2_ref[...], axis=-1, keepdims=True)) + lam_init)
    att = a1_ref[...] / l1_ref[...] - lam * (a2_ref[...] / l2_ref[...])
    inv = lax.rsqrt(jnp.mean(att * att, axis=-1, keepdims=True) + SUBLN_EPS)
    o_ref[...] = (att * inv * w_ref[...] * (1.0 - lam_init)).astype(o_ref.dtype)


def differential_attention(proj, near, far, lam_vecs, subln_w, *, row_start, batch, seq, heads, lam_init):
    t = near.shape[-1]
    assert seq % t == 0 and row_start % seq == 0
    nq = seq // t
    qoff = row_start // t
    soff = row_start // seq
    hd2 = 2 * HEAD_DIM
    vec = pl.BlockSpec((1, HEAD_DIM), lambda b, h, i: (0, 0))
    vmem = 2 * 2 * seq * hd2 * 2 + 2 * 3 * t * t * 4 + 8 * t * t * 4 + 8 * t * hd2 * 4 + (4 << 20)
    return pl.pallas_call(
        functools.partial(_diff_attn_kernel, nkv=nq, t=t, lam_init=lam_init),
        out_shape=jax.ShapeDtypeStruct((batch * seq, heads * hd2), BF16),
        grid=(batch, heads, nq),
        in_specs=[pl.BlockSpec(memory_space=pltpu.SMEM),
                  pl.BlockSpec((t, hd2), lambda b, h, i: (qoff + b * nq + i, h)),
                  pl.BlockSpec((seq, hd2), lambda b, h, i: (soff + b, heads + h)),
                  pl.BlockSpec((seq, hd2), lambda b, h, i: (soff + b, 2 * heads + h)),
                  pl.BlockSpec((None, 3, t, t), lambda b, h, i: (h, 0, 0, 0)),
                  vec, vec, vec, vec,
                  pl.BlockSpec((1, hd2), lambda b, h, i: (0, 0))],
        out_specs=pl.BlockSpec((t, hd2), lambda b, h, i: (b * nq + i, h)),
        scratch_shapes=[pltpu.VMEM((t, 1), F32), pltpu.VMEM((t, 1), F32), pltpu.VMEM((t, hd2), F32),
                        pltpu.VMEM((t, 1), F32), pltpu.VMEM((t, 1), F32), pltpu.VMEM((t, hd2), F32)],
        compiler_params=_params(("parallel", "parallel", "arbitrary"), vmem),
        name="differential_attention",
    )(far, proj, proj, proj, near, *lam_vecs, subln_w.reshape(1, hd2).astype(F32))


def _pad_axis(w, axis, mult):
    pad = (-w.shape[axis]) % mult
    if pad == 0:
        return w
    widths = [(0, 0)] * w.ndim
    widths[axis] = (0, pad)
    return jnp.pad(w, widths)


def kernel(x_prompt, x_sample, norm_mix, norm_ffn, w_in_even, rpb_na, w_out_even, w_in_odd,
           lambda_q1, lambda_k1, lambda_q2, lambda_k2, subln_w, w_out_odd, t5_bias,
           w_gate, w_up, w_down, norm_final):
    d_model = x_prompt.shape[-1]
    depth = norm_mix.shape[0]
    segs = []
    start = 0
    for xb in (x_prompt, x_sample):
        segs.append((start, xb.shape[0], xb.shape[1]))
        start += xb.shape[0] * xb.shape[1]
    x = jnp.concatenate([x_prompt.reshape(-1, d_model), x_sample.reshape(-1, d_model)], axis=0)

    na_heads = rpb_na.shape[1]
    na_width = na_heads * HEAD_DIM
    diff_heads = t5_bias.shape[1]
    fnet_width = w_in_even.shape[-1] - 3 * na_width
    group_dim = fnet_width // FNET_GROUPS

    for layer in range(depth):
        h = rmsnorm(x, norm_mix[layer], BF16)
        if layer % 2 == 0:
            e = layer // 2
            w_in = w_in_even[e].astype(BF16)
            qkv = matmul(h, w_in[:, :3 * na_width], BF16)
            u = matmul(h, w_in[:, 3 * na_width:], BF16)
            mixed = []
            for row_start, batch, seq in segs:
                kh = min(NA_ROWS, seq // GRID_W)
                oa = neighbourhood_attention(qkv, _na_bias_table(rpb_na[e], kh), row_start=row_start,
                                             batch=batch, seq=seq, heads=na_heads)
                ob = fourier_mix(u, row_start=row_start, batch=batch, seq=seq, group_dim=group_dim)
                mixed.append(jnp.concatenate([oa, ob], axis=-1))
            x = matmul(jnp.concatenate(mixed, axis=0), w_out_even[e].astype(BF16), F32, residual=x)
        else:
            o = layer // 2
            proj = matmul(h, w_in_odd[o].astype(BF16), BF16)
            lam_init = 0.8 - 0.6 * math.exp(-0.3 * layer)
            near, far = _t5_near_bias(t5_bias, ATT_TILE)
            lam_vecs = [v[o].reshape(1, HEAD_DIM).astype(F32)
                        for v in (lambda_q1, lambda_k1, lambda_q2, lambda_k2)]
            att = [differential_attention(proj, near, far, lam_vecs, subln_w[o], row_start=row_start, batch=batch,
                                          seq=seq, heads=diff_heads, lam_init=lam_init)
                   for row_start, batch, seq in segs]
            x = matmul(jnp.concatenate(att, axis=0), w_out_odd[o].astype(BF16), F32, residual=x)
        h = rmsnorm(x, norm_ffn[layer], BF16)
        wg = _pad_axis(w_gate[layer].astype(BF16), 1, FFN_PAD)
        wu = _pad_axis(w_up[layer].astype(BF16), 1, FFN_PAD)
        wd = _pad_axis(w_down[layer].astype(BF16), 0, FFN_PAD)
        act = gate_up(h, wg, wu)
        x = matmul(act, wd, F32, residual=x, bn=2048, tk=512)

    outs = []
    for (row_start, batch, seq), xb in zip(segs, (x_prompt, x_sample)):
        y = rmsnorm(x, norm_final, F32, row_start=row_start, rows=batch * seq)
        outs.append(y.reshape(xb.shape))
    return tuple(outs)
```

```python
import functools
import math

import jax
import jax.numpy as jnp
from jax import lax
from jax.experimental import pallas as pl
from jax.experimental.pallas import tpu as pltpu

F32 = jnp.float32
BF16 = jnp.bfloat16

HEAD_DIM = 128
GRID_W = 64
NA_ROWS = 8
NA_COLS = 16
FNET_GROUPS = 4
T5_BUCKETS = 32
T5_MAX_DIST = 128
RMS_EPS = 1e-6
SUBLN_EPS = 1e-5
NEG_BIG = -1e30

V7X_VMEM_LIMIT = 56 * 1024 * 1024
ATT_R = 256
ATT_C = 256
ATT_TQ = 512
NA_UNROLL = 4
FFN_PAD = 1024
LOG2E = math.log2(math.e)


def _toeplitz(w, rows, cols):
    lead = w.shape[:-1]
    period = rows + cols
    wp = jnp.concatenate([w, jnp.zeros(lead + (1,), w.dtype)], axis=-1)
    flat = jnp.tile(wp, (1,) * len(lead) + (rows,))[..., :rows * (period - 1)]
    return flat.reshape(lead + (rows, period - 1))[..., rows - 1:rows - 1 + cols]


def _params(sem, vmem_bytes):
    return pltpu.CompilerParams(dimension_semantics=sem,
                                vmem_limit_bytes=int(min(V7X_VMEM_LIMIT, vmem_bytes * 3 // 2)))


def _rmsnorm_kernel(x_ref, g_ref, o_ref, *, eps):
    x = x_ref[...]
    inv = lax.rsqrt(jnp.mean(x * x, axis=-1, keepdims=True) + eps)
    o_ref[...] = (x * inv * g_ref[...]).astype(o_ref.dtype)


def rmsnorm(x, g, out_dtype, *, row_start=0, rows=None, bm=256):
    total, d = x.shape
    rows = total if rows is None else rows
    off = row_start // bm
    assert row_start % bm == 0 and rows % bm == 0
    return pl.pallas_call(
        functools.partial(_rmsnorm_kernel, eps=RMS_EPS),
        out_shape=jax.ShapeDtypeStruct((rows, d), out_dtype),
        grid=(rows // bm,),
        in_specs=[pl.BlockSpec((bm, d), lambda i: (i + off, 0)),
                  pl.BlockSpec((1, d), lambda i: (0, 0))],
        out_specs=pl.BlockSpec((bm, d), lambda i: (i, 0)),
        compiler_params=_params(("parallel",), 6 * bm * d * 4),
        name="rmsnorm",
    )(x, g.reshape(1, d).astype(F32))


def _matmul_kernel(*refs, nk, has_res):
    if has_res:
        a_ref, w_ref, r_ref, o_ref = refs[:4]
    else:
        a_ref, w_ref, o_ref = refs[:3]
        r_ref = None
    def part():
        return jnp.dot(a_ref[...], w_ref[...], preferred_element_type=F32)

    if nk == 1:
        o_ref[...] = (part() + r_ref[...] if r_ref is not None else part()).astype(o_ref.dtype)
        return
    k = pl.program_id(2)

    @pl.when(k == 0)
    def _():
        o_ref[...] = part() + r_ref[...] if r_ref is not None else part()

    @pl.when(k > 0)
    def _():
        o_ref[...] += part()


def matmul(a, w, out_dtype, *, residual=None, bm=1024, bn=1024, tk=None):
    m, kdim = a.shape
    _, n = w.shape
    tk = kdim if tk is None else tk
    bm, bn = min(bm, m), min(bn, n)
    assert m % bm == 0 and n % bn == 0 and kdim % tk == 0
    nk = kdim // tk
    has_res = residual is not None
    in_specs = [pl.BlockSpec((bm, tk), lambda i, j, k: (i, k)),
                pl.BlockSpec((tk, bn), lambda i, j, k: (k, j))]
    args = [a, w]
    if has_res:
        in_specs.append(pl.BlockSpec((bm, bn), lambda i, j, k: (i, j)))
        args.append(residual)
    assert nk == 1 or out_dtype == F32
    out_bytes = jnp.dtype(out_dtype).itemsize
    vmem = (2 * (bm * tk + tk * bn) * 2 + 2 * bm * bn * out_bytes + (2 * bm * bn * 4 if has_res else 0)
            + bm * bn * 4)
    return pl.pallas_call(
        functools.partial(_matmul_kernel, nk=nk, has_res=has_res),
        out_shape=jax.ShapeDtypeStruct((m, n), out_dtype),
        grid=(m // bm, n // bn, nk),
        in_specs=in_specs,
        out_specs=pl.BlockSpec((bm, bn), lambda i, j, k: (i, j)),
        compiler_params=_params(("parallel", "parallel", "arbitrary"), vmem),
        name="matmul",
    )(*args)


def _gate_up_kernel(a_ref, wg_ref, wu_ref, o_ref):
    a = a_ref[...]
    g = jnp.dot(a, wg_ref[...], preferred_element_type=F32)
    u = jnp.dot(a, wu_ref[...], preferred_element_type=F32)
    o_ref[...] = (g * jax.nn.sigmoid(g) * u).astype(o_ref.dtype)


def gate_up(h, wg, wu, *, bm=1024, bn=512):
    m, kdim = h.shape
    _, n = wg.shape
    assert m % bm == 0 and n % bn == 0
    vmem = 2 * bm * kdim * 2 + 4 * kdim * bn * 2 + 2 * bm * bn * 2 + 4 * bm * bn * 4
    return pl.pallas_call(
        _gate_up_kernel,
        out_shape=jax.ShapeDtypeStruct((m, n), BF16),
        grid=(m // bm, n // bn),
        in_specs=[pl.BlockSpec((bm, kdim), lambda i, j: (i, 0)),
                  pl.BlockSpec((kdim, bn), lambda i, j: (0, j)),
                  pl.BlockSpec((kdim, bn), lambda i, j: (0, j))],
        out_specs=pl.BlockSpec((bm, bn), lambda i, j: (i, j)),
        compiler_params=_params(("parallel", "parallel"), vmem),
        name="gate_up",
    )(h, wg, wu)


def _na_kernel(q_ref, k_ref, v_ref, bias_ref, o_ref, *, rows, kh):
    scale = HEAD_DIM ** -0.5
    win = kh * GRID_W

    def one_row(r):
        rs = jnp.clip(r - kh // 2, 0, rows - kh)
        qoff = pl.multiple_of(r * GRID_W, GRID_W)
        koff = pl.multiple_of(rs * GRID_W, GRID_W)
        q = q_ref[pl.ds(qoff, GRID_W), :]
        kw = k_ref[pl.ds(koff, win), :]
        vw = v_ref[pl.ds(koff, win), :]
        s = lax.dot_general(q, kw, (((1,), (1,)), ((), ())), preferred_element_type=F32)
        s = s * scale + bias_ref[r - rs]
        p = jnp.exp(s - jnp.max(s, axis=-1, keepdims=True))
        l = jnp.sum(p, axis=-1, keepdims=True)
        o = jnp.dot(p.astype(vw.dtype), vw, preferred_element_type=F32) / l
        o_ref[pl.ds(qoff, GRID_W), :] = o.astype(o_ref.dtype)

    def row_group(g, carry):
        for t in range(NA_UNROLL):
            one_row(g * NA_UNROLL + t)
        return carry

    assert rows % NA_UNROLL == 0
    lax.fori_loop(0, rows // NA_UNROLL, row_group, 0)


def _na_bias_table(rpb, kh):
    heads = rpb.shape[0]
    c = jnp.arange(GRID_W, dtype=jnp.int32)
    cs = jnp.clip(c - NA_COLS // 2, 0, GRID_W - NA_COLS)
    j = jnp.arange(GRID_W, dtype=jnp.int32)
    valid = (j[None, :] >= cs[:, None]) & (j[None, :] < cs[:, None] + NA_COLS)
    pad = GRID_W - NA_COLS
    w = jnp.pad(rpb.astype(F32), ((0, 0), (0, 0), (pad, pad)))
    t = jnp.where(valid, _toeplitz(w, GRID_W, GRID_W), NEG_BIG)
    t = jnp.transpose(t, (0, 2, 1, 3))
    tabs = [t[:, :, NA_ROWS - 1 - dl:NA_ROWS - 1 - dl + kh, :].reshape(heads, GRID_W, kh * GRID_W)
            for dl in range(kh)]
    return jnp.stack(tabs, axis=1)


def neighbourhood_attention(qkv, bias_tab, *, row_start, batch, seq, heads):
    rows = seq // GRID_W
    kh = bias_tab.shape[1]
    off = row_start // seq
    assert row_start % seq == 0
    vmem = 2 * 4 * seq * HEAD_DIM * 2 + 2 * bias_tab[0].size * 4 + (8 << 20)
    return pl.pallas_call(
        functools.partial(_na_kernel, rows=rows, kh=kh),
        out_shape=jax.ShapeDtypeStruct((batch * seq, heads * HEAD_DIM), BF16),
        grid=(batch, heads),
        in_specs=[pl.BlockSpec((seq, HEAD_DIM), lambda b, h: (b + off, h)),
                  pl.BlockSpec((seq, HEAD_DIM), lambda b, h: (b + off, heads + h)),
                  pl.BlockSpec((seq, HEAD_DIM), lambda b, h: (b + off, 2 * heads + h)),
                  pl.BlockSpec((None, kh, GRID_W, kh * GRID_W), lambda b, h: (h, 0, 0, 0))],
        out_specs=pl.BlockSpec((seq, HEAD_DIM), lambda b, h: (b, h)),
        compiler_params=_params(("parallel", "parallel"), vmem),
        name="neighbourhood_attention",
    )(qkv, qkv, qkv, bias_tab)


def _dft_tables(n):
    idx = jnp.arange(n, dtype=jnp.int32)
    prod = (idx[:, None] * idx[None, :]) % n
    ang = prod.astype(F32) * (2.0 * math.pi / n)
    norm = n ** -0.5
    return (jnp.cos(ang) * norm).astype(BF16), (jnp.sin(ang) * norm).astype(BF16)


def _seq_dft_kernel(c_ref, s_ref, y1_ref, y2_ref, o_ref, acc_ref, *, nk):
    k = pl.program_id(2)
    part = (jnp.dot(c_ref[...], y1_ref[...], preferred_element_type=F32)
            - jnp.dot(s_ref[...], y2_ref[...], preferred_element_type=F32))

    @pl.when(k == 0)
    def _():
        acc_ref[...] = part

    @pl.when(k > 0)
    def _():
        acc_ref[...] += part

    @pl.when(k == nk - 1)
    def _():
        o_ref[...] = acc_ref[...].astype(o_ref.dtype)


def fourier_mix(u, *, row_start, batch, seq, group_dim, bm=1024, tk=1024):
    total, width = u.shape
    groups = width // group_dim
    cg, sg = _dft_tables(group_dim)
    ug = u.reshape(total * groups, group_dim)
    y1 = matmul(ug, cg, BF16, bm=2048, bn=group_dim).reshape(total, width)
    y2 = matmul(ug, sg, BF16, bm=2048, bn=group_dim).reshape(total, width)
    cs, ss = _dft_tables(seq)
    bm, tk = min(bm, seq), min(tk, seq)
    nk = seq // tk
    off = row_start // tk
    assert row_start % tk == 0
    vmem = 2 * 2 * bm * tk * 2 + 2 * 2 * tk * width * 2 + 2 * bm * width * 2 + 3 * bm * width * 4
    return pl.pallas_call(
        functools.partial(_seq_dft_kernel, nk=nk),
        out_shape=jax.ShapeDtypeStruct((batch * seq, width), BF16),
        grid=(batch, seq // bm, nk),
        in_specs=[pl.BlockSpec((bm, tk), lambda b, i, k: (i, k)),
                  pl.BlockSpec((bm, tk), lambda b, i, k: (i, k)),
                  pl.BlockSpec((tk, width), lambda b, i, k: (off + b * nk + k, 0)),
                  pl.BlockSpec((tk, width), lambda b, i, k: (off + b * nk + k, 0))],
        out_specs=pl.BlockSpec((bm, width), lambda b, i, k: (b * (seq // bm) + i, 0)),
        scratch_shapes=[pltpu.VMEM((bm, width), F32)],
        compiler_params=_params(("parallel", "parallel", "arbitrary"), vmem),
        name="seq_dft",
    )(cs, ss, y1, y2)


def _t5_bucket(rel):
    nb = T5_BUCKETS // 2
    max_exact = nb // 2
    ret = (rel > 0).astype(jnp.int32) * nb
    n = jnp.abs(rel)
    nf = jnp.maximum(n, 1).astype(F32)
    large = max_exact + (jnp.log(nf / max_exact) / math.log(T5_MAX_DIST / max_exact)
                         * (nb - max_exact)).astype(jnp.int32)
    large = jnp.minimum(large, nb - 1)
    return ret + jnp.where(n < max_exact, n, large)


ATT_BIAS_CHUNKS = (-2, -1, 0, 1, 2)


def _t5_bias_tiles(t5_bias):
    assert ATT_C == ATT_R and ATT_R + 1 >= T5_MAX_DIST
    tiles = []
    for dc in ATT_BIAS_CHUNKS:
        rel = dc * ATT_C + ATT_C - 1 - jnp.arange(ATT_C + ATT_R - 1, dtype=jnp.int32)
        w = t5_bias.astype(F32)[_t5_bucket(rel)].T * LOG2E
        tiles.append(_toeplitz(w, ATT_C, ATT_R))
    return jnp.stack(tiles, axis=1)


def _diff_attn_kernel(q_ref, k_ref, vt_ref, nb_ref, lq1_ref, lk1_ref, lq2_ref, lk2_ref, w_ref, o_ref,
                      m_ref, l_ref, a_ref, s_ref, p_ref, al_ref, *, nc, nsub, lam_init):
    i = pl.program_id(2)
    d = HEAD_DIM
    m_ref[...] = jnp.full(m_ref.shape, NEG_BIG, F32)
    l_ref[...] = jnp.zeros(l_ref.shape, F32)
    a_ref[...] = jnp.zeros(a_ref.shape, F32)
    lo, hi = ATT_BIAS_CHUNKS[0], ATT_BIAS_CHUNKS[-1]

    def scores(c):
        kc = k_ref[pl.ds(pl.multiple_of(c * ATT_C, ATT_C), ATT_C), :]
        for r in range(nsub):
            bias = nb_ref[jnp.clip(c - (i * nsub + r), lo, hi) - lo]
            for half in (0, 1):
                q = q_ref[r * ATT_R:(r + 1) * ATT_R, half * d:(half + 1) * d]
                s_ref[c % 2, r, half] = lax.dot_general(
                    kc[:, half * d:(half + 1) * d], q, (((1,), (1,)), ((), ())),
                    preferred_element_type=F32) + bias

    def softmax(c):
        for r in range(nsub):
            for half in (0, 1):
                s = s_ref[c % 2, r, half]
                m_old = m_ref[r, half]
                m_new = jnp.maximum(m_old, jnp.max(s, axis=0, keepdims=True))
                p = jnp.exp2(s - m_new)
                alpha = jnp.exp2(m_old - m_new)
                l_ref[r, half] = alpha * l_ref[r, half] + jnp.sum(p, axis=0, keepdims=True)
                m_ref[r, half] = m_new
                al_ref[c % 2, r, half] = alpha
                p_ref[c % 2, r, half] = p.astype(p_ref.dtype)

    def accumulate(c):
        vt = vt_ref[c]
        for r in range(nsub):
            for half in (0, 1):
                a_ref[r, half] = (al_ref[c % 2, r, half] * a_ref[r, half]
                                  + jnp.dot(vt, p_ref[c % 2, r, half], preferred_element_type=F32))

    scores(0)
    scores(1)
    softmax(0)

    def step(c, carry):
        accumulate(c - 2)
        softmax(c - 1)
        scores(c)
        return carry

    lax.fori_loop(2, nc, step, 0)
    accumulate(nc - 2)
    softmax(nc - 1)
    accumulate(nc - 1)

    lam = (jnp.exp(jnp.sum(lq1_ref[...] * lk1_ref[...], axis=-1, keepdims=True))
           - jnp.exp(jnp.sum(lq2_ref[...] * lk2_ref[...], axis=-1, keepdims=True)) + lam_init)
    for r in range(nsub):
        att = a_ref[r, 0] / l_ref[r, 0] - lam * (a_ref[r, 1] / l_ref[r, 1])
        inv = lax.rsqrt(jnp.mean(att * att, axis=0, keepdims=True) + SUBLN_EPS)
        y = (att * inv).T * w_ref[...] * (1.0 - lam_init)
        o_ref[r * ATT_R:(r + 1) * ATT_R, :] = y.astype(o_ref.dtype)


def differential_attention(proj, vt, bias_tiles, lam_vecs, subln_w, *, row_start, batch, seq, heads, lam_init):
    tq = min(ATT_TQ, seq)
    assert seq % tq == 0 and row_start % seq == 0 and tq % ATT_R == 0 and seq >= 2 * ATT_C
    nq = seq // tq
    nc = seq // ATT_C
    nsub = tq // ATT_R
    qoff = row_start // tq
    soff = row_start // seq
    hd2 = 2 * HEAD_DIM
    vec = pl.BlockSpec((1, HEAD_DIM), lambda b, h, i: (0, 0))
    vmem = (2 * 2 * seq * hd2 * 2 + 2 * bias_tiles[0].size * 4 + 4 * tq * hd2 * 2
            + 2 * nsub * ATT_R * ((hd2 + 4 * 8) * 4 + 2 * ATT_C * 6))
    return pl.pallas_call(
        functools.partial(_diff_attn_kernel, nc=nc, nsub=nsub, lam_init=lam_init),
        out_shape=jax.ShapeDtypeStruct((batch * seq, heads * hd2), BF16),
        grid=(batch, heads, nq),
        in_specs=[pl.BlockSpec((tq, hd2), lambda b, h, i: (qoff + b * nq + i, h)),
                  pl.BlockSpec((seq, hd2), lambda b, h, i: (soff + b, heads + h)),
                  pl.BlockSpec((None, nc, hd2, ATT_C), lambda b, h, i: (h, soff + b, 0, 0)),
                  pl.BlockSpec((None,) + bias_tiles.shape[1:], lambda b, h, i: (h, 0, 0, 0)),
                  vec, vec, vec, vec,
                  pl.BlockSpec((1, hd2), lambda b, h, i: (0, 0))],
        out_specs=pl.BlockSpec((tq, hd2), lambda b, h, i: (b * nq + i, h)),
        scratch_shapes=[pltpu.VMEM((nsub, 2, 1, ATT_R), F32), pltpu.VMEM((nsub, 2, 1, ATT_R), F32),
                        pltpu.VMEM((nsub, 2, hd2, ATT_R), F32), pltpu.VMEM((2, nsub, 2, ATT_C, ATT_R), F32),
                        pltpu.VMEM((2, nsub, 2, ATT_C, ATT_R), BF16), pltpu.VMEM((2, nsub, 2, 1, ATT_R), F32)],
        compiler_params=_params(("parallel", "parallel", "arbitrary"), vmem),
        name="differential_attention",
    )(proj, proj, vt, bias_tiles, *lam_vecs, subln_w.reshape(1, hd2).astype(F32))


def _pad_axis(w, axis, mult):
    pad = (-w.shape[axis]) % mult
    if pad == 0:
        return w
    widths = [(0, 0)] * w.ndim
    widths[axis] = (0, pad)
    return jnp.pad(w, widths)


def kernel(x_prompt, x_sample, norm_mix, norm_ffn, w_in_even, rpb_na, w_out_even, w_in_odd,
           lambda_q1, lambda_k1, lambda_q2, lambda_k2, subln_w, w_out_odd, t5_bias,
           w_gate, w_up, w_down, norm_final):
    d_model = x_prompt.shape[-1]
    depth = norm_mix.shape[0]
    segs = []
    start = 0
    for xb in (x_prompt, x_sample):
        segs.append((start, xb.shape[0], xb.shape[1]))
        start += xb.shape[0] * xb.shape[1]
    x = jnp.concatenate([x_prompt.reshape(-1, d_model), x_sample.reshape(-1, d_model)], axis=0)

    na_heads = rpb_na.shape[1]
    na_width = na_heads * HEAD_DIM
    diff_heads = t5_bias.shape[1]
    fnet_width = w_in_even.shape[-1] - 3 * na_width
    group_dim = fnet_width // FNET_GROUPS

    for layer in range(depth):
        h = rmsnorm(x, norm_mix[layer], BF16)
        if layer % 2 == 0:
            e = layer // 2
            w_in = w_in_even[e].astype(BF16)
            qkv = matmul(h, w_in[:, :3 * na_width], BF16)
            u = matmul(h, w_in[:, 3 * na_width:], BF16)
            mixed = []
            for row_start, batch, seq in segs:
                kh = min(NA_ROWS, seq // GRID_W)
                oa = neighbourhood_attention(qkv, _na_bias_table(rpb_na[e], kh), row_start=row_start,
                                             batch=batch, seq=seq, heads=na_heads)
                ob = fourier_mix(u, row_start=row_start, batch=batch, seq=seq, group_dim=group_dim)
                mixed.append(jnp.concatenate([oa, ob], axis=-1))
            x = matmul(jnp.concatenate(mixed, axis=0), w_out_even[e].astype(BF16), F32, residual=x)
        else:
            o = layer // 2
            diff_width = diff_heads * 2 * HEAD_DIM
            col_scale = jnp.where(jnp.arange(3 * diff_width) < diff_width, HEAD_DIM ** -0.5 * LOG2E, 1.0)
            proj = matmul(h, (w_in_odd[o] * col_scale.astype(F32)).astype(BF16), BF16)
            vt = proj[:, 2 * diff_width:].reshape(-1, ATT_C, diff_heads, 2 * HEAD_DIM)
            vt = jnp.transpose(vt, (2, 0, 3, 1))
            lam_init = 0.8 - 0.6 * math.exp(-0.3 * layer)
            bias_tiles = _t5_bias_tiles(t5_bias)
            lam_vecs = [v[o].reshape(1, HEAD_DIM).astype(F32)
                        for v in (lambda_q1, lambda_k1, lambda_q2, lambda_k2)]
            att = [differential_attention(proj, vt, bias_tiles, lam_vecs, subln_w[o], row_start=row_start,
                                          batch=batch, seq=seq, heads=diff_heads, lam_init=lam_init)
                   for row_start, batch, seq in segs]
            x = matmul(jnp.concatenate(att, axis=0), w_out_odd[o].astype(BF16), F32, residual=x)
        h = rmsnorm(x, norm_ffn[layer], BF16)
        wg = _pad_axis(w_gate[layer].astype(BF16), 1, FFN_PAD)
        wu = _pad_axis(w_up[layer].astype(BF16), 1, FFN_PAD)
        wd = _pad_axis(w_down[layer].astype(BF16), 0, FFN_PAD)
        act = gate_up(h, wg, wu)
        x = matmul(act, wd, F32, residual=x, bn=2048, tk=1024)

    outs = []
    for (row_start, batch, seq), xb in zip(segs, (x_prompt, x_sample)):
        y = rmsnorm(x, norm_final, F32, row_start=row_start, rows=batch * seq)
        outs.append(y.reshape(xb.shape))
    return tuple(outs)
```

```python
import functools
import math

import jax
import jax.numpy as jnp
from jax import lax
from jax.experimental import pallas as pl
from jax.experimental.pallas import tpu as pltpu

F32 = jnp.float32
BF16 = jnp.bfloat16

HEAD_DIM = 128
GRID_W = 64
NA_ROWS = 8
NA_COLS = 16
FNET_GROUPS = 4
T5_BUCKETS = 32
T5_MAX_DIST = 128
RMS_EPS = 1e-6
SUBLN_EPS = 1e-5
NEG_BIG = -1e30

V7X_VMEM_LIMIT = 56 * 1024 * 1024
ATT_R = 256
ATT_C = 256
ATT_TQ = 512
NA_UNROLL = 8
FFN_PAD = 1024
LOG2E = math.log2(math.e)


def _toeplitz(w, rows, cols):
    lead = w.shape[:-1]
    period = rows + cols
    wp = jnp.concatenate([w, jnp.zeros(lead + (1,), w.dtype)], axis=-1)
    flat = jnp.tile(wp, (1,) * len(lead) + (rows,))[..., :rows * (period - 1)]
    return flat.reshape(lead + (rows, period - 1))[..., rows - 1:rows - 1 + cols]


def _params(sem, vmem_bytes):
    return pltpu.CompilerParams(dimension_semantics=sem,
                                vmem_limit_bytes=int(min(V7X_VMEM_LIMIT, vmem_bytes * 3 // 2)))


def _rmsnorm_kernel(x_ref, g_ref, o_ref, *, eps):
    x = x_ref[...]
    inv = lax.rsqrt(jnp.mean(x * x, axis=-1, keepdims=True) + eps)
    o_ref[...] = (x * inv * g_ref[...]).astype(o_ref.dtype)


def rmsnorm(x, g, out_dtype, *, row_start=0, rows=None, bm=256):
    total, d = x.shape
    rows = total if rows is None else rows
    off = row_start // bm
    assert row_start % bm == 0 and rows % bm == 0
    return pl.pallas_call(
        functools.partial(_rmsnorm_kernel, eps=RMS_EPS),
        out_shape=jax.ShapeDtypeStruct((rows, d), out_dtype),
        grid=(rows // bm,),
        in_specs=[pl.BlockSpec((bm, d), lambda i: (i + off, 0)),
                  pl.BlockSpec((1, d), lambda i: (0, 0))],
        out_specs=pl.BlockSpec((bm, d), lambda i: (i, 0)),
        compiler_params=_params(("parallel",), 6 * bm * d * 4),
        name="rmsnorm",
    )(x, g.reshape(1, d).astype(F32))


def _matmul_kernel(*refs, nk, has_res):
    if has_res:
        a_ref, w_ref, r_ref, o_ref = refs[:4]
    else:
        a_ref, w_ref, o_ref = refs[:3]
        r_ref = None
    def part():
        return jnp.dot(a_ref[...], w_ref[...], preferred_element_type=F32)

    if nk == 1:
        o_ref[...] = (part() + r_ref[...] if r_ref is not None else part()).astype(o_ref.dtype)
        return
    k = pl.program_id(2)

    @pl.when(k == 0)
    def _():
        o_ref[...] = part() + r_ref[...] if r_ref is not None else part()

    @pl.when(k > 0)
    def _():
        o_ref[...] += part()


def matmul(a, w, out_dtype, *, residual=None, bm=1024, bn=1024, tk=None):
    m, kdim = a.shape
    _, n = w.shape
    tk = kdim if tk is None else tk
    bm, bn = min(bm, m), min(bn, n)
    assert m % bm == 0 and n % bn == 0 and kdim % tk == 0
    nk = kdim // tk
    has_res = residual is not None
    in_specs = [pl.BlockSpec((bm, tk), lambda i, j, k: (i, k)),
                pl.BlockSpec((tk, bn), lambda i, j, k: (k, j))]
    args = [a, w]
    if has_res:
        in_specs.append(pl.BlockSpec((bm, bn), lambda i, j, k: (i, j)))
        args.append(residual)
    assert nk == 1 or out_dtype == F32
    out_bytes = jnp.dtype(out_dtype).itemsize
    vmem = (2 * (bm * tk + tk * bn) * 2 + 2 * bm * bn * out_bytes + (2 * bm * bn * 4 if has_res else 0)
            + bm * bn * 4)
    return pl.pallas_call(
        functools.partial(_matmul_kernel, nk=nk, has_res=has_res),
        out_shape=jax.ShapeDtypeStruct((m, n), out_dtype),
        grid=(m // bm, n // bn, nk),
        in_specs=in_specs,
        out_specs=pl.BlockSpec((bm, bn), lambda i, j, k: (i, j)),
        compiler_params=_params(("parallel", "parallel", "arbitrary"), vmem),
        name="matmul",
    )(*args)


def _gate_up_kernel(a_ref, wg_ref, wu_ref, o_ref):
    a = a_ref[...]
    g = jnp.dot(a, wg_ref[...], preferred_element_type=F32)
    u = jnp.dot(a, wu_ref[...], preferred_element_type=F32)
    o_ref[...] = (g * jax.nn.sigmoid(g) * u).astype(o_ref.dtype)


def gate_up(h, wg, wu, *, bm=1024, bn=512):
    m, kdim = h.shape
    _, n = wg.shape
    assert m % bm == 0 and n % bn == 0
    vmem = 2 * bm * kdim * 2 + 4 * kdim * bn * 2 + 2 * bm * bn * 2 + 4 * bm * bn * 4
    return pl.pallas_call(
        _gate_up_kernel,
        out_shape=jax.ShapeDtypeStruct((m, n), BF16),
        grid=(m // bm, n // bn),
        in_specs=[pl.BlockSpec((bm, kdim), lambda i, j: (i, 0)),
                  pl.BlockSpec((kdim, bn), lambda i, j: (0, j)),
                  pl.BlockSpec((kdim, bn), lambda i, j: (0, j))],
        out_specs=pl.BlockSpec((bm, bn), lambda i, j: (i, j)),
        compiler_params=_params(("parallel", "parallel"), vmem),
        name="gate_up",
    )(h, wg, wu)


def _na_kernel(q_ref, k_ref, v_ref, bias_ref, o_ref, *, rows, kh):
    scale = HEAD_DIM ** -0.5
    win = kh * GRID_W

    def row_group(g, carry):
        rws = [g * NA_UNROLL + t for t in range(NA_UNROLL)]
        starts = [jnp.clip(r - kh // 2, 0, rows - kh) for r in rws]
        qoffs = [pl.multiple_of(r * GRID_W, GRID_W) for r in rws]
        koffs = [pl.multiple_of(rs * GRID_W, GRID_W) for rs in starts]
        ss = [lax.dot_general(q_ref[pl.ds(qo, GRID_W), :], k_ref[pl.ds(ko, win), :], (((1,), (1,)), ((), ())),
                              preferred_element_type=F32) for qo, ko in zip(qoffs, koffs)]
        ps, ls = [], []
        for s, r, rs in zip(ss, rws, starts):
            s = s * scale + bias_ref[r - rs]
            p = jnp.exp(s - jnp.max(s, axis=-1, keepdims=True))
            ls.append(jnp.sum(p, axis=-1, keepdims=True))
            ps.append(p.astype(v_ref.dtype))
        os_ = [jnp.dot(p, v_ref[pl.ds(ko, win), :], preferred_element_type=F32) for p, ko in zip(ps, koffs)]
        for o, l, qo in zip(os_, ls, qoffs):
            o_ref[pl.ds(qo, GRID_W), :] = (o / l).astype(o_ref.dtype)
        return carry

    assert rows % NA_UNROLL == 0
    lax.fori_loop(0, rows // NA_UNROLL, row_group, 0)


def _na_bias_table(rpb, kh):
    heads = rpb.shape[0]
    c = jnp.arange(GRID_W, dtype=jnp.int32)
    cs = jnp.clip(c - NA_COLS // 2, 0, GRID_W - NA_COLS)
    j = jnp.arange(GRID_W, dtype=jnp.int32)
    valid = (j[None, :] >= cs[:, None]) & (j[None, :] < cs[:, None] + NA_COLS)
    pad = GRID_W - NA_COLS
    w = jnp.pad(rpb.astype(F32), ((0, 0), (0, 0), (pad, pad)))
    t = jnp.where(valid, _toeplitz(w, GRID_W, GRID_W), NEG_BIG)
    t = jnp.transpose(t, (0, 2, 1, 3))
    tabs = [t[:, :, NA_ROWS - 1 - dl:NA_ROWS - 1 - dl + kh, :].reshape(heads, GRID_W, kh * GRID_W)
            for dl in range(kh)]
    return jnp.stack(tabs, axis=1)


def neighbourhood_attention(qkv, bias_tab, *, row_start, batch, seq, heads):
    rows = seq // GRID_W
    kh = bias_tab.shape[1]
    off = row_start // seq
    assert row_start % seq == 0
    vmem = 2 * 4 * seq * HEAD_DIM * 2 + 2 * bias_tab[0].size * 4 + (8 << 20)
    return pl.pallas_call(
        functools.partial(_na_kernel, rows=rows, kh=kh),
        out_shape=jax.ShapeDtypeStruct((batch * seq, heads * HEAD_DIM), BF16),
        grid=(batch, heads),
        in_specs=[pl.BlockSpec((seq, HEAD_DIM), lambda b, h: (b + off, h)),
                  pl.BlockSpec((seq, HEAD_DIM), lambda b, h: (b + off, heads + h)),
                  pl.BlockSpec((seq, HEAD_DIM), lambda b, h: (b + off, 2 * heads + h)),
                  pl.BlockSpec((None, kh, GRID_W, kh * GRID_W), lambda b, h: (h, 0, 0, 0))],
        out_specs=pl.BlockSpec((seq, HEAD_DIM), lambda b, h: (b, h)),
        compiler_params=_params(("parallel", "parallel"), vmem),
        name="neighbourhood_attention",
    )(qkv, qkv, qkv, bias_tab)


DFT_N2 = 64
DFT_CB = 8
LANES = 128


def _cos_sin(num, den):
    ang = (num % den).astype(F32) * (2.0 * math.pi / den)
    return jnp.cos(ang), jnp.sin(ang)


def _dft_stage1_kernel(w_ref, x_ref, o_ref):
    o_ref[...] = jnp.dot(w_ref[...], x_ref[...], preferred_element_type=F32).astype(o_ref.dtype)


def _dft_stage2_kernel(tr_ref, ti_ref, tc_ref, ts_ref, w2_ref, wg_ref, o_ref, *, groups, gdim):
    cb, n2, width = tr_ref.shape
    rep = width // LANES
    xr, xi = [], []
    for c in range(cb):
        tr = tr_ref[c].astype(F32)
        ti = ti_ref[c].astype(F32)
        tc = jnp.tile(tc_ref[c], (1, rep))
        ts = jnp.tile(ts_ref[c], (1, rep))
        stacked = jnp.concatenate([tr * tc + ti * ts, ti * tc - tr * ts], axis=0).astype(BF16)
        x = jnp.dot(w2_ref[...], stacked, preferred_element_type=F32)
        xr.append(x[:n2])
        xi.append(x[n2:])
    xr = jnp.concatenate(xr, axis=0).astype(BF16)
    xi = jnp.concatenate(xi, axis=0).astype(BF16)
    for g in range(groups):
        cols = slice(g * gdim, (g + 1) * gdim)
        f = (jnp.dot(xr[:, cols], wg_ref[0], preferred_element_type=F32)
             + jnp.dot(xi[:, cols], wg_ref[1], preferred_element_type=F32))
        for c in range(cb):
            o_ref[:, c * width + g * gdim:c * width + (g + 1) * gdim] = f[c * n2:(c + 1) * n2].astype(o_ref.dtype)


def fourier_mix(u, *, row_start, batch, seq, group_dim):
    total, width = u.shape
    groups = width // group_dim
    n2 = DFT_N2
    n1 = seq // n2
    cb = min(DFT_CB, n1)
    assert seq % n2 == 0 and n1 % cb == 0 and row_start % seq == 0 and total % n2 == 0
    row = n2 * width
    bn = min(row, 8192)
    a = jnp.arange(n1, dtype=jnp.int32)
    b = jnp.arange(n2, dtype=jnp.int32)
    c1, s1 = _cos_sin(a[:, None] * a[None, :], n1)
    w1 = (jnp.concatenate([c1, -s1], axis=0) * n1 ** -0.5).astype(BF16)
    off = row_start // seq
    t = pl.pallas_call(
        _dft_stage1_kernel,
        out_shape=jax.ShapeDtypeStruct((batch, 2 * n1, row), BF16),
        grid=(batch, row // bn),
        in_specs=[pl.BlockSpec((2 * n1, n1), lambda bi, j: (0, 0)),
                  pl.BlockSpec((n1, bn), lambda bi, j: (off + bi, j))],
        out_specs=pl.BlockSpec((None, 2 * n1, bn), lambda bi, j: (bi, 0, j)),
        compiler_params=_params(("parallel", "parallel"), 2 * (3 * n1 * bn * 2 + 2 * n1 * n1 * 2) + 2 * n1 * bn * 4),
        name="dft_stage1",
    )(w1, u.reshape(total // n2, row))
    t = t.reshape(batch, 2 * n1, n2, width)
    tc, ts = _cos_sin(a[:, None] * b[None, :], seq)
    tc = jnp.broadcast_to(tc[:, :, None], (n1, n2, LANES))
    ts = jnp.broadcast_to(ts[:, :, None], (n1, n2, LANES))
    c2, s2 = _cos_sin(b[:, None] * b[None, :], n2)
    w2 = (jnp.block([[c2, s2], [-s2, c2]]) * n2 ** -0.5).astype(BF16)
    gi = jnp.arange(group_dim, dtype=jnp.int32)
    cg, sg = _cos_sin(gi[:, None] * gi[None, :], group_dim)
    wg = (jnp.stack([cg, sg]) * group_dim ** -0.5).astype(BF16)
    nj = n1 // cb
    vmem = (2 * (2 * cb * n2 * width * 2 + 2 * cb * n2 * LANES * 4 + n2 * cb * width * 2) + 4 * n2 * n2 * 2
            + 2 * wg.size * 2 + 6 * cb * n2 * width * 4)
    out = pl.pallas_call(
        functools.partial(_dft_stage2_kernel, groups=groups, gdim=group_dim),
        out_shape=jax.ShapeDtypeStruct((batch, n2, n1 * width), BF16),
        grid=(batch, nj),
        in_specs=[pl.BlockSpec((None, cb, n2, width), lambda bi, j: (bi, j, 0, 0)),
                  pl.BlockSpec((None, cb, n2, width), lambda bi, j: (bi, nj + j, 0, 0)),
                  pl.BlockSpec((cb, n2, LANES), lambda bi, j: (j, 0, 0)),
                  pl.BlockSpec((cb, n2, LANES), lambda bi, j: (j, 0, 0)),
                  pl.BlockSpec((2 * n2, 2 * n2), lambda bi, j: (0, 0)),
                  pl.BlockSpec((2, group_dim, group_dim), lambda bi, j: (0, 0, 0))],
        out_specs=pl.BlockSpec((None, n2, cb * width), lambda bi, j: (bi, 0, j)),
        compiler_params=_params(("parallel", "parallel"), vmem),
        name="dft_stage2",
    )(t, t, tc, ts, w2, wg)
    return out.reshape(batch * seq, width)


def _t5_bucket(rel):
    nb = T5_BUCKETS // 2
    max_exact = nb // 2
    ret = (rel > 0).astype(jnp.int32) * nb
    n = jnp.abs(rel)
    nf = jnp.maximum(n, 1).astype(F32)
    large = max_exact + (jnp.log(nf / max_exact) / math.log(T5_MAX_DIST / max_exact)
                         * (nb - max_exact)).astype(jnp.int32)
    large = jnp.minimum(large, nb - 1)
    return ret + jnp.where(n < max_exact, n, large)


ATT_BIAS_CHUNKS = (-2, -1, 0, 1, 2)


def _t5_bias_tiles(t5_bias):
    assert ATT_C == ATT_R and ATT_R + 1 >= T5_MAX_DIST
    tiles = []
    for dc in ATT_BIAS_CHUNKS:
        rel = dc * ATT_C + ATT_C - 1 - jnp.arange(ATT_C + ATT_R - 1, dtype=jnp.int32)
        w = t5_bias.astype(F32)[_t5_bucket(rel)].T * LOG2E
        tiles.append(_toeplitz(w, ATT_C, ATT_R))
    return jnp.stack(tiles, axis=1)


ATT_SUM_ROWS = 16


def _diff_attn_kernel(qt_ref, k_ref, vt_ref, nb_ref, lq1_ref, lk1_ref, lq2_ref, lk2_ref, w_ref, o_ref,
                      m_ref, a_ref, s_ref, p_ref, al_ref, *, nc, nsub, lam_init):
    i = pl.program_id(2)
    d = HEAD_DIM
    m_ref[...] = jnp.full(m_ref.shape, NEG_BIG, F32)
    a_ref[...] = jnp.zeros(a_ref.shape, F32)
    lo, hi = ATT_BIAS_CHUNKS[0], ATT_BIAS_CHUNKS[-1]
    units = [(r, half) for r in range(nsub) for half in (0, 1)]

    def scores(c, slot):
        for r, half in units:
            kc = k_ref[pl.ds(pl.multiple_of(c * ATT_C, ATT_C), ATT_C), half * d:(half + 1) * d]
            bias = nb_ref[jnp.clip(c - (i * nsub + r), lo, hi) - lo]
            qt = qt_ref[half * d:(half + 1) * d, r * ATT_R:(r + 1) * ATT_R]
            s_ref[slot, r, half] = jnp.dot(kc, qt, preferred_element_type=F32) + bias

    def softmax(slot):
        for r, half in units:
            s = s_ref[slot, r, half]
            m_old = m_ref[r, half]
            m_new = jnp.maximum(m_old, jnp.max(s, axis=0, keepdims=True))
            m_ref[r, half] = m_new
            al_ref[slot, r, half] = jnp.exp2(m_old - m_new)
            p_ref[slot, r, half] = jnp.exp2(s - m_new).astype(p_ref.dtype)

    def accumulate(c, slot):
        for r, half in units:
            a_ref[r, half] = (al_ref[slot, r, half] * a_ref[r, half]
                              + jnp.dot(vt_ref[c], p_ref[slot, r, half], preferred_element_type=F32))

    scores(0, 0)
    scores(1, 1)
    softmax(0)

    def step(j, carry):
        c = 2 * j
        accumulate(c - 2, 0)
        softmax(1)
        scores(c, 0)
        accumulate(c - 1, 1)
        softmax(0)
        scores(c + 1, 1)
        return carry

    lax.fori_loop(1, nc // 2, step, 0)
    accumulate(nc - 2, 0)
    softmax(1)
    accumulate(nc - 1, 1)

    lam = (jnp.exp(jnp.sum(lq1_ref[...] * lk1_ref[...], axis=-1, keepdims=True))
           - jnp.exp(jnp.sum(lq2_ref[...] * lk2_ref[...], axis=-1, keepdims=True)) + lam_init)
    hd2 = 2 * d
    for r in range(nsub):
        o1, o2 = (a_ref[r, half, :hd2] / a_ref[r, half, hd2:hd2 + 1] for half in (0, 1))
        att = o1 - lam * o2
        inv = lax.rsqrt(jnp.mean(att * att, axis=0, keepdims=True) + SUBLN_EPS)
        y = (att * inv).T * w_ref[...] * (1.0 - lam_init)
        o_ref[r * ATT_R:(r + 1) * ATT_R, :] = y.astype(o_ref.dtype)


def differential_attention(qt, proj, vt, bias_tiles, lam_vecs, subln_w, *, row_start, batch, seq, heads, lam_init):
    tq = min(ATT_TQ, seq)
    nq = seq // tq
    nc = seq // ATT_C
    nsub = tq // ATT_R
    assert seq % tq == 0 and row_start % seq == 0 and tq % ATT_R == 0 and nc % 2 == 0
    qoff = row_start // tq
    soff = row_start // seq
    hd2 = 2 * HEAD_DIM
    rows_a = hd2 + ATT_SUM_ROWS
    vec = pl.BlockSpec((1, HEAD_DIM), lambda b, h, i: (0, 0))
    vmem = (2 * seq * (hd2 + rows_a) * 2 + 2 * bias_tiles[0].size * 4 + 4 * tq * hd2 * 2
            + 2 * nsub * ATT_R * ((rows_a + 3 * 8) * 4 + 2 * ATT_C * 6))
    return pl.pallas_call(
        functools.partial(_diff_attn_kernel, nc=nc, nsub=nsub, lam_init=lam_init),
        out_shape=jax.ShapeDtypeStruct((batch * seq, heads * hd2), BF16),
        grid=(batch, heads, nq),
        in_specs=[pl.BlockSpec((None, hd2, tq), lambda b, h, i: (h, 0, qoff + b * nq + i)),
                  pl.BlockSpec((seq, hd2), lambda b, h, i: (soff + b, heads + h)),
                  pl.BlockSpec((None, nc, rows_a, ATT_C), lambda b, h, i: (h, soff + b, 0, 0)),
                  pl.BlockSpec((None,) + bias_tiles.shape[1:], lambda b, h, i: (h, 0, 0, 0)),
                  vec, vec, vec, vec,
                  pl.BlockSpec((1, hd2), lambda b, h, i: (0, 0))],
        out_specs=pl.BlockSpec((tq, hd2), lambda b, h, i: (b * nq + i, h)),
        scratch_shapes=[pltpu.VMEM((nsub, 2, 1, ATT_R), F32), pltpu.VMEM((nsub, 2, rows_a, ATT_R), F32),
                        pltpu.VMEM((2, nsub, 2, ATT_C, ATT_R), F32), pltpu.VMEM((2, nsub, 2, ATT_C, ATT_R), BF16),
                        pltpu.VMEM((2, nsub, 2, 1, ATT_R), F32)],
        compiler_params=_params(("parallel", "parallel", "arbitrary"), vmem),
        name="differential_attention",
    )(qt, proj, vt, bias_tiles, *lam_vecs, subln_w.reshape(1, hd2).astype(F32))


def _pad_axis(w, axis, mult):
    pad = (-w.shape[axis]) % mult
    if pad == 0:
        return w
    widths = [(0, 0)] * w.ndim
    widths[axis] = (0, pad)
    return jnp.pad(w, widths)


def kernel(x_prompt, x_sample, norm_mix, norm_ffn, w_in_even, rpb_na, w_out_even, w_in_odd,
           lambda_q1, lambda_k1, lambda_q2, lambda_k2, subln_w, w_out_odd, t5_bias,
           w_gate, w_up, w_down, norm_final):
    d_model = x_prompt.shape[-1]
    depth = norm_mix.shape[0]
    segs = []
    start = 0
    for xb in (x_prompt, x_sample):
        segs.append((start, xb.shape[0], xb.shape[1]))
        start += xb.shape[0] * xb.shape[1]
    x = jnp.concatenate([x_prompt.reshape(-1, d_model), x_sample.reshape(-1, d_model)], axis=0)

    na_heads = rpb_na.shape[1]
    na_width = na_heads * HEAD_DIM
    diff_heads = t5_bias.shape[1]
    fnet_width = w_in_even.shape[-1] - 3 * na_width
    group_dim = fnet_width // FNET_GROUPS

    for layer in range(depth):
        h = rmsnorm(x, norm_mix[layer], BF16)
        if layer % 2 == 0:
            e = layer // 2
            w_in = w_in_even[e].astype(BF16)
            qkv = matmul(h, w_in[:, :3 * na_width], BF16)
            u = matmul(h, w_in[:, 3 * na_width:], BF16)
            mixed = []
            for row_start, batch, seq in segs:
                kh = min(NA_ROWS, seq // GRID_W)
                oa = neighbourhood_attention(qkv, _na_bias_table(rpb_na[e], kh), row_start=row_start,
                                             batch=batch, seq=seq, heads=na_heads)
                ob = fourier_mix(u, row_start=row_start, batch=batch, seq=seq, group_dim=group_dim)
                mixed.append(jnp.concatenate([oa, ob], axis=-1))
            x = matmul(jnp.concatenate(mixed, axis=0), w_out_even[e].astype(BF16), F32, residual=x)
        else:
            o = layer // 2
            diff_width = diff_heads * 2 * HEAD_DIM
            col_scale = jnp.where(jnp.arange(3 * diff_width) < diff_width, HEAD_DIM ** -0.5 * LOG2E, 1.0)
            proj = matmul(h, (w_in_odd[o] * col_scale.astype(F32)).astype(BF16), BF16)
            qt = jnp.transpose(proj[:, :diff_width].reshape(-1, diff_heads, 2 * HEAD_DIM), (1, 2, 0))
            vt = proj[:, 2 * diff_width:].reshape(-1, ATT_C, diff_heads, 2 * HEAD_DIM)
            vt = jnp.transpose(vt, (2, 0, 3, 1))
            vt = jnp.concatenate([vt, jnp.ones(vt.shape[:2] + (ATT_SUM_ROWS, ATT_C), BF16)], axis=2)
            lam_init = 0.8 - 0.6 * math.exp(-0.3 * layer)
            bias_tiles = _t5_bias_tiles(t5_bias)
            lam_vecs = [v[o].reshape(1, HEAD_DIM).astype(F32)
                        for v in (lambda_q1, lambda_k1, lambda_q2, lambda_k2)]
            att = [differential_attention(qt, proj, vt, bias_tiles, lam_vecs, subln_w[o], row_start=row_start,
                                          batch=batch, seq=seq, heads=diff_heads, lam_init=lam_init)
                   for row_start, batch, seq in segs]
            x = matmul(jnp.concatenate(att, axis=0), w_out_odd[o].astype(BF16), F32, residual=x)
        h = rmsnorm(x, norm_ffn[layer], BF16)
        wg = _pad_axis(w_gate[layer].astype(BF16), 1, FFN_PAD)
        wu = _pad_axis(w_up[layer].astype(BF16), 1, FFN_PAD)
        wd = _pad_axis(w_down[layer].astype(BF16), 0, FFN_PAD)
        act = gate_up(h, wg, wu)
        x = matmul(act, wd, F32, residual=x, bn=2048, tk=1024)

    outs = []
    for (row_start, batch, seq), xb in zip(segs, (x_prompt, x_sample)):
        y = rmsnorm(x, norm_final, F32, row_start=row_start, rows=batch * seq)
        outs.append(y.reshape(xb.shape))
    return tuple(outs)
```

```python
import functools
import math

import jax
import jax.numpy as jnp
from jax import lax
from jax.experimental import pallas as pl
from jax.experimental.pallas import tpu as pltpu

F32 = jnp.float32
BF16 = jnp.bfloat16

HEAD_DIM = 128
GRID_W = 64
NA_ROWS = 8
NA_COLS = 16
FNET_GROUPS = 4
T5_BUCKETS = 32
T5_MAX_DIST = 128
RMS_EPS = 1e-6
SUBLN_EPS = 1e-5
NEG_BIG = -1e30

V7X_VMEM_LIMIT = 56 * 1024 * 1024
ATT_R = 256
ATT_C = 256
ATT_TQ = 512
ATT_UNROLL = 4
NA_UNROLL = 8
FFN_PAD = 1024
LOG2E = math.log2(math.e)


def _toeplitz(w, rows, cols):
    lead = w.shape[:-1]
    period = rows + cols
    wp = jnp.concatenate([w, jnp.zeros(lead + (1,), w.dtype)], axis=-1)
    flat = jnp.tile(wp, (1,) * len(lead) + (rows,))[..., :rows * (period - 1)]
    return flat.reshape(lead + (rows, period - 1))[..., rows - 1:rows - 1 + cols]


def _params(sem, vmem_bytes):
    return pltpu.CompilerParams(dimension_semantics=sem,
                                vmem_limit_bytes=int(min(V7X_VMEM_LIMIT, vmem_bytes * 3 // 2)))


def _rmsnorm_kernel(x_ref, g_ref, o_ref, *, eps):
    x = x_ref[...]
    inv = lax.rsqrt(jnp.mean(x * x, axis=-1, keepdims=True) + eps)
    o_ref[...] = (x * inv * g_ref[...]).astype(o_ref.dtype)


def rmsnorm(x, g, out_dtype, *, row_start=0, rows=None, bm=256):
    total, d = x.shape
    rows = total if rows is None else rows
    off = row_start // bm
    assert row_start % bm == 0 and rows % bm == 0
    return pl.pallas_call(
        functools.partial(_rmsnorm_kernel, eps=RMS_EPS),
        out_shape=jax.ShapeDtypeStruct((rows, d), out_dtype),
        grid=(rows // bm,),
        in_specs=[pl.BlockSpec((bm, d), lambda i: (i + off, 0)),
                  pl.BlockSpec((1, d), lambda i: (0, 0))],
        out_specs=pl.BlockSpec((bm, d), lambda i: (i, 0)),
        compiler_params=_params(("parallel",), 6 * bm * d * 4),
        name="rmsnorm",
    )(x, g.reshape(1, d).astype(F32))


def _matmul_kernel(*refs, nk, na, has_res):
    a_refs, w_ref = refs[:na], refs[na]
    r_ref = refs[na + 1] if has_res else None
    o_ref = refs[-1]

    def part():
        acc, start = None, 0
        for a_ref in a_refs:
            stop = start + a_ref.shape[1]
            d = jnp.dot(a_ref[...], w_ref[start:stop, :], preferred_element_type=F32)
            acc = d if acc is None else acc + d
            start = stop
        return acc

    if nk == 1:
        o_ref[...] = (part() + r_ref[...] if r_ref is not None else part()).astype(o_ref.dtype)
        return
    k = pl.program_id(2)

    @pl.when(k == 0)
    def _():
        o_ref[...] = part() + r_ref[...] if r_ref is not None else part()

    @pl.when(k > 0)
    def _():
        o_ref[...] += part()


def matmul(a, w, out_dtype, *, residual=None, bm=1024, bn=1024, tk=None):
    a_list = list(a) if isinstance(a, (list, tuple)) else [a]
    m = a_list[0].shape[0]
    kdim, n = w.shape
    assert sum(x.shape[1] for x in a_list) == kdim
    tk = kdim if tk is None else tk
    bm, bn = min(bm, m), min(bn, n)
    assert m % bm == 0 and n % bn == 0 and kdim % tk == 0
    nk = kdim // tk
    assert nk == 1 or len(a_list) == 1
    has_res = residual is not None
    in_specs = [pl.BlockSpec((bm, x.shape[1] // nk), lambda i, j, k: (i, k)) for x in a_list]
    in_specs.append(pl.BlockSpec((tk, bn), lambda i, j, k: (k, j)))
    args = a_list + [w]
    if has_res:
        in_specs.append(pl.BlockSpec((bm, bn), lambda i, j, k: (i, j)))
        args.append(residual)
    assert nk == 1 or out_dtype == F32
    out_bytes = jnp.dtype(out_dtype).itemsize
    vmem = (2 * (bm * tk + tk * bn) * 2 + 2 * bm * bn * out_bytes + (2 * bm * bn * 4 if has_res else 0)
            + bm * bn * 4)
    return pl.pallas_call(
        functools.partial(_matmul_kernel, nk=nk, na=len(a_list), has_res=has_res),
        out_shape=jax.ShapeDtypeStruct((m, n), out_dtype),
        grid=(m // bm, n // bn, nk),
        in_specs=in_specs,
        out_specs=pl.BlockSpec((bm, bn), lambda i, j, k: (i, j)),
        compiler_params=_params(("parallel", "parallel", "arbitrary"), vmem),
        name="matmul",
    )(*args)


def _gate_up_kernel(a_ref, wg_ref, wu_ref, o_ref):
    a = a_ref[...]
    g = jnp.dot(a, wg_ref[...], preferred_element_type=F32)
    u = jnp.dot(a, wu_ref[...], preferred_element_type=F32)
    o_ref[...] = (g * jax.nn.sigmoid(g) * u).astype(o_ref.dtype)


def gate_up(h, wg, wu, *, bm=1024, bn=512):
    m, kdim = h.shape
    _, n = wg.shape
    assert m % bm == 0 and n % bn == 0
    vmem = 2 * bm * kdim * 2 + 4 * kdim * bn * 2 + 2 * bm * bn * 2 + 4 * bm * bn * 4
    return pl.pallas_call(
        _gate_up_kernel,
        out_shape=jax.ShapeDtypeStruct((m, n), BF16),
        grid=(m // bm, n // bn),
        in_specs=[pl.BlockSpec((bm, kdim), lambda i, j: (i, 0)),
                  pl.BlockSpec((kdim, bn), lambda i, j: (0, j)),
                  pl.BlockSpec((kdim, bn), lambda i, j: (0, j))],
        out_specs=pl.BlockSpec((bm, bn), lambda i, j: (i, j)),
        compiler_params=_params(("parallel", "parallel"), vmem),
        name="gate_up",
    )(h, wg, wu)


def _na_kernel(q_ref, k_ref, v_ref, bias_ref, o_ref, *, rows, kh):
    scale = HEAD_DIM ** -0.5
    win = kh * GRID_W

    def row_group(g, carry):
        rws = [g * NA_UNROLL + t for t in range(NA_UNROLL)]
        starts = [jnp.clip(r - kh // 2, 0, rows - kh) for r in rws]
        qoffs = [pl.multiple_of(r * GRID_W, GRID_W) for r in rws]
        koffs = [pl.multiple_of(rs * GRID_W, GRID_W) for rs in starts]
        ss = [lax.dot_general(q_ref[pl.ds(qo, GRID_W), :], k_ref[pl.ds(ko, win), :], (((1,), (1,)), ((), ())),
                              preferred_element_type=F32) for qo, ko in zip(qoffs, koffs)]
        ps, ls = [], []
        for s, r, rs in zip(ss, rws, starts):
            s = s * scale + bias_ref[r - rs]
            p = jnp.exp(s - jnp.max(s, axis=-1, keepdims=True))
            ls.append(jnp.sum(p, axis=-1, keepdims=True))
            ps.append(p.astype(v_ref.dtype))
        os_ = [jnp.dot(p, v_ref[pl.ds(ko, win), :], preferred_element_type=F32) for p, ko in zip(ps, koffs)]
        for o, l, qo in zip(os_, ls, qoffs):
            o_ref[pl.ds(qo, GRID_W), :] = (o / l).astype(o_ref.dtype)
        return carry

    assert rows % NA_UNROLL == 0
    lax.fori_loop(0, rows // NA_UNROLL, row_group, 0)


def _na_bias_table(rpb, kh):
    heads = rpb.shape[0]
    c = jnp.arange(GRID_W, dtype=jnp.int32)
    cs = jnp.clip(c - NA_COLS // 2, 0, GRID_W - NA_COLS)
    j = jnp.arange(GRID_W, dtype=jnp.int32)
    valid = (j[None, :] >= cs[:, None]) & (j[None, :] < cs[:, None] + NA_COLS)
    pad = GRID_W - NA_COLS
    w = jnp.pad(rpb.astype(F32), ((0, 0), (0, 0), (pad, pad)))
    t = jnp.where(valid, _toeplitz(w, GRID_W, GRID_W), NEG_BIG)
    t = jnp.transpose(t, (0, 2, 1, 3))
    tabs = [t[:, :, NA_ROWS - 1 - dl:NA_ROWS - 1 - dl + kh, :].reshape(heads, GRID_W, kh * GRID_W)
            for dl in range(kh)]
    return jnp.stack(tabs, axis=1)


def neighbourhood_attention(qkv, bias_tab, *, row_start, batch, seq, heads):
    rows = seq // GRID_W
    kh = bias_tab.shape[1]
    off = row_start // seq
    assert row_start % seq == 0
    vmem = 2 * 4 * seq * HEAD_DIM * 2 + 2 * bias_tab[0].size * 4 + (8 << 20)
    return pl.pallas_call(
        functools.partial(_na_kernel, rows=rows, kh=kh),
        out_shape=jax.ShapeDtypeStruct((batch * seq, heads * HEAD_DIM), BF16),
        grid=(batch, heads),
        in_specs=[pl.BlockSpec((seq, HEAD_DIM), lambda b, h: (b + off, h)),
                  pl.BlockSpec((seq, HEAD_DIM), lambda b, h: (b + off, heads + h)),
                  pl.BlockSpec((seq, HEAD_DIM), lambda b, h: (b + off, 2 * heads + h)),
                  pl.BlockSpec((None, kh, GRID_W, kh * GRID_W), lambda b, h: (h, 0, 0, 0))],
        out_specs=pl.BlockSpec((seq, HEAD_DIM), lambda b, h: (b, h)),
        compiler_params=_params(("parallel", "parallel"), vmem),
        name="neighbourhood_attention",
    )(qkv, qkv, qkv, bias_tab)


DFT_N2 = 64
DFT_CB = 8
LANES = 128


def _cos_sin(num, den):
    ang = (num % den).astype(F32) * (2.0 * math.pi / den)
    return jnp.cos(ang), jnp.sin(ang)


def _dft_stage1_kernel(w_ref, x_ref, o_ref):
    o_ref[...] = jnp.dot(w_ref[...], x_ref[...], preferred_element_type=F32).astype(o_ref.dtype)


def _dft_stage2_kernel(tr_ref, ti_ref, tc_ref, ts_ref, w2_ref, wg_ref, o_ref, *, groups, gdim):
    cb, n2, width = tr_ref.shape
    rep = width // LANES
    xr, xi = [], []
    for c in range(cb):
        tr = tr_ref[c].astype(F32)
        ti = ti_ref[c].astype(F32)
        tc = jnp.tile(tc_ref[c], (1, rep))
        ts = jnp.tile(ts_ref[c], (1, rep))
        stacked = jnp.concatenate([tr * tc + ti * ts, ti * tc - tr * ts], axis=0).astype(BF16)
        x = jnp.dot(w2_ref[...], stacked, preferred_element_type=F32)
        xr.append(x[:n2])
        xi.append(x[n2:])
    xr = jnp.concatenate(xr, axis=0).astype(BF16)
    xi = jnp.concatenate(xi, axis=0).astype(BF16)
    for g in range(groups):
        cols = slice(g * gdim, (g + 1) * gdim)
        f = (jnp.dot(xr[:, cols], wg_ref[0], preferred_element_type=F32)
             + jnp.dot(xi[:, cols], wg_ref[1], preferred_element_type=F32))
        for c in range(cb):
            o_ref[:, c * width + g * gdim:c * width + (g + 1) * gdim] = f[c * n2:(c + 1) * n2].astype(o_ref.dtype)


def fourier_mix(u, *, row_start, batch, seq, group_dim):
    total, width = u.shape
    groups = width // group_dim
    n2 = DFT_N2
    n1 = seq // n2
    cb = min(DFT_CB, n1)
    assert seq % n2 == 0 and n1 % cb == 0 and row_start % seq == 0 and total % n2 == 0
    row = n2 * width
    bn = min(row, 8192)
    a = jnp.arange(n1, dtype=jnp.int32)
    b = jnp.arange(n2, dtype=jnp.int32)
    c1, s1 = _cos_sin(a[:, None] * a[None, :], n1)
    w1 = (jnp.concatenate([c1, -s1], axis=0) * n1 ** -0.5).astype(BF16)
    off = row_start // seq
    t = pl.pallas_call(
        _dft_stage1_kernel,
        out_shape=jax.ShapeDtypeStruct((batch, 2 * n1, row), BF16),
        grid=(batch, row // bn),
        in_specs=[pl.BlockSpec((2 * n1, n1), lambda bi, j: (0, 0)),
                  pl.BlockSpec((n1, bn), lambda bi, j: (off + bi, j))],
        out_specs=pl.BlockSpec((None, 2 * n1, bn), lambda bi, j: (bi, 0, j)),
        compiler_params=_params(("parallel", "parallel"), 2 * (3 * n1 * bn * 2 + 2 * n1 * n1 * 2) + 2 * n1 * bn * 4),
        name="dft_stage1",
    )(w1, u.reshape(total // n2, row))
    t = t.reshape(batch, 2 * n1, n2, width)
    tc, ts = _cos_sin(a[:, None] * b[None, :], seq)
    tc = jnp.broadcast_to(tc[:, :, None], (n1, n2, LANES))
    ts = jnp.broadcast_to(ts[:, :, None], (n1, n2, LANES))
    c2, s2 = _cos_sin(b[:, None] * b[None, :], n2)
    w2 = (jnp.block([[c2, s2], [-s2, c2]]) * n2 ** -0.5).astype(BF16)
    gi = jnp.arange(group_dim, dtype=jnp.int32)
    cg, sg = _cos_sin(gi[:, None] * gi[None, :], group_dim)
    wg = (jnp.stack([cg, sg]) * group_dim ** -0.5).astype(BF16)
    nj = n1 // cb
    vmem = (2 * (2 * cb * n2 * width * 2 + 2 * cb * n2 * LANES * 4 + n2 * cb * width * 2) + 4 * n2 * n2 * 2
            + 2 * wg.size * 2 + 6 * cb * n2 * width * 4)
    out = pl.pallas_call(
        functools.partial(_dft_stage2_kernel, groups=groups, gdim=group_dim),
        out_shape=jax.ShapeDtypeStruct((batch, n2, n1 * width), BF16),
        grid=(batch, nj),
        in_specs=[pl.BlockSpec((None, cb, n2, width), lambda bi, j: (bi, j, 0, 0)),
                  pl.BlockSpec((None, cb, n2, width), lambda bi, j: (bi, nj + j, 0, 0)),
                  pl.BlockSpec((cb, n2, LANES), lambda bi, j: (j, 0, 0)),
                  pl.BlockSpec((cb, n2, LANES), lambda bi, j: (j, 0, 0)),
                  pl.BlockSpec((2 * n2, 2 * n2), lambda bi, j: (0, 0)),
                  pl.BlockSpec((2, group_dim, group_dim), lambda bi, j: (0, 0, 0))],
        out_specs=pl.BlockSpec((None, n2, cb * width), lambda bi, j: (bi, 0, j)),
        compiler_params=_params(("parallel", "parallel"), vmem),
        name="dft_stage2",
    )(t, t, tc, ts, w2, wg)
    return out.reshape(batch * seq, width)


def _t5_bucket(rel):
    nb = T5_BUCKETS // 2
    max_exact = nb // 2
    ret = (rel > 0).astype(jnp.int32) * nb
    n = jnp.abs(rel)
    nf = jnp.maximum(n, 1).astype(F32)
    large = max_exact + (jnp.log(nf / max_exact) / math.log(T5_MAX_DIST / max_exact)
                         * (nb - max_exact)).astype(jnp.int32)
    large = jnp.minimum(large, nb - 1)
    return ret + jnp.where(n < max_exact, n, large)


ATT_BIAS_CHUNKS = (-2, -1, 0, 1, 2)


def _t5_bias_tiles(t5_bias):
    assert ATT_C == ATT_R and ATT_R + 1 >= T5_MAX_DIST
    tiles = []
    for dc in ATT_BIAS_CHUNKS:
        rel = dc * ATT_C + ATT_C - 1 - jnp.arange(ATT_C + ATT_R - 1, dtype=jnp.int32)
        w = t5_bias.astype(F32)[_t5_bucket(rel)].T * LOG2E
        tiles.append(_toeplitz(w, ATT_C, ATT_R))
    return jnp.stack(tiles, axis=1)


ATT_SUM_ROWS = 16


def _diff_attn_kernel(qt_ref, k_ref, vt_ref, nb_ref, lq1_ref, lk1_ref, lq2_ref, lk2_ref, w_ref, o_ref,
                      m_ref, a_ref, s_ref, p_ref, al_ref, *, nc, nsub, lam_init):
    i = pl.program_id(2)
    d = HEAD_DIM
    m_ref[...] = jnp.full(m_ref.shape, NEG_BIG, F32)
    a_ref[...] = jnp.zeros(a_ref.shape, F32)
    lo, hi = ATT_BIAS_CHUNKS[0], ATT_BIAS_CHUNKS[-1]
    units = [(r, half) for r in range(nsub) for half in (0, 1)]

    def scores(c, slot):
        for r, half in units:
            kc = k_ref[pl.ds(pl.multiple_of(c * ATT_C, ATT_C), ATT_C), half * d:(half + 1) * d]
            bias = nb_ref[jnp.clip(c - (i * nsub + r), lo, hi) - lo]
            qt = qt_ref[half * d:(half + 1) * d, r * ATT_R:(r + 1) * ATT_R]
            s_ref[slot, r, half] = jnp.dot(kc, qt, preferred_element_type=F32) + bias

    def softmax(slot):
        for r, half in units:
            s = s_ref[slot, r, half]
            m_old = m_ref[r, half]
            m_new = jnp.maximum(m_old, jnp.max(s, axis=0, keepdims=True))
            m_ref[r, half] = m_new
            al_ref[slot, r, half] = jnp.exp2(m_old - m_new)
            p_ref[slot, r, half] = jnp.exp2(s - m_new).astype(p_ref.dtype)

    def accumulate(c, slot):
        for r, half in units:
            a_ref[r, half] = (al_ref[slot, r, half] * a_ref[r, half]
                              + jnp.dot(vt_ref[c], p_ref[slot, r, half], preferred_element_type=F32))

    def advance(c, t):
        accumulate(c - 2, t)
        softmax(1 - t)
        scores(c, t)

    scores(0, 0)
    scores(1, 1)
    softmax(0)
    for c in range(2, ATT_UNROLL):
        advance(c, c % 2)

    def step(j, carry):
        for t in range(ATT_UNROLL):
            advance(ATT_UNROLL * j + t, t % 2)
        return carry

    lax.fori_loop(1, nc // ATT_UNROLL, step, 0)
    accumulate(nc - 2, 0)
    softmax(1)
    accumulate(nc - 1, 1)

    lam = (jnp.exp(jnp.sum(lq1_ref[...] * lk1_ref[...], axis=-1, keepdims=True))
           - jnp.exp(jnp.sum(lq2_ref[...] * lk2_ref[...], axis=-1, keepdims=True)) + lam_init)
    hd2 = 2 * d
    for r in range(nsub):
        o1, o2 = (a_ref[r, half, :hd2] / a_ref[r, half, hd2:hd2 + 1] for half in (0, 1))
        att = o1 - lam * o2
        inv = lax.rsqrt(jnp.mean(att * att, axis=0, keepdims=True) + SUBLN_EPS)
        y = (att * inv).T * w_ref[...] * (1.0 - lam_init)
        o_ref[r * ATT_R:(r + 1) * ATT_R, :] = y.astype(o_ref.dtype)


def differential_attention(qt, proj, vt, bias_tiles, lam_vecs, subln_w, *, row_start, batch, seq, heads, lam_init):
    tq = min(ATT_TQ, seq)
    nq = seq // tq
    nc = seq // ATT_C
    nsub = tq // ATT_R
    assert seq % tq == 0 and row_start % seq == 0 and tq % ATT_R == 0 and nc % ATT_UNROLL == 0
    qoff = row_start // tq
    soff = row_start // seq
    hd2 = 2 * HEAD_DIM
    rows_a = hd2 + ATT_SUM_ROWS
    vec = pl.BlockSpec((1, HEAD_DIM), lambda b, h, i: (0, 0))
    vmem = (2 * seq * (hd2 + rows_a) * 2 + 2 * bias_tiles[0].size * 4 + 4 * tq * hd2 * 2
            + 2 * nsub * ATT_R * ((rows_a + 3 * 8) * 4 + 2 * ATT_C * 6))
    return pl.pallas_call(
        functools.partial(_diff_attn_kernel, nc=nc, nsub=nsub, lam_init=lam_init),
        out_shape=jax.ShapeDtypeStruct((batch * seq, heads * hd2), BF16),
        grid=(batch, heads, nq),
        in_specs=[pl.BlockSpec((None, hd2, tq), lambda b, h, i: (h, 0, qoff + b * nq + i)),
                  pl.BlockSpec((seq, hd2), lambda b, h, i: (soff + b, heads + h)),
                  pl.BlockSpec((None, nc, rows_a, ATT_C), lambda b, h, i: (h, soff + b, 0, 0)),
                  pl.BlockSpec((None,) + bias_tiles.shape[1:], lambda b, h, i: (h, 0, 0, 0)),
                  vec, vec, vec, vec,
                  pl.BlockSpec((1, hd2), lambda b, h, i: (0, 0))],
        out_specs=pl.BlockSpec((tq, hd2), lambda b, h, i: (b * nq + i, h)),
        scratch_shapes=[pltpu.VMEM((nsub, 2, 1, ATT_R), F32), pltpu.VMEM((nsub, 2, rows_a, ATT_R), F32),
                        pltpu.VMEM((2, nsub, 2, ATT_C, ATT_R), F32), pltpu.VMEM((2, nsub, 2, ATT_C, ATT_R), BF16),
                        pltpu.VMEM((2, nsub, 2, 1, ATT_R), F32)],
        compiler_params=_params(("parallel", "parallel", "arbitrary"), vmem),
        name="differential_attention",
    )(qt, proj, vt, bias_tiles, *lam_vecs, subln_w.reshape(1, hd2).astype(F32))


def _cast_pad_kernel(x_ref, o_ref, *, row_blocks, pad_rows):
    cols = x_ref.shape[1]

    def copy():
        o_ref[:, :cols] = x_ref[...].astype(o_ref.dtype)
        if cols < o_ref.shape[1]:
            o_ref[:, cols:] = jnp.zeros((o_ref.shape[0], o_ref.shape[1] - cols), o_ref.dtype)

    if not pad_rows:
        copy()
    else:
        pl.when(pl.program_id(0) < row_blocks)(copy)

        @pl.when(pl.program_id(0) >= row_blocks)
        def _():
            o_ref[...] = jnp.zeros(o_ref.shape, o_ref.dtype)


def cast_pad(w, rows_to, cols_to, *, bm=LANES):
    rows, cols = w.shape
    assert rows % bm == 0 and rows_to % bm == 0 and cols % LANES == 0 and cols_to % LANES == 0
    row_blocks = rows // bm
    return pl.pallas_call(
        functools.partial(_cast_pad_kernel, row_blocks=row_blocks, pad_rows=rows_to > rows),
        out_shape=jax.ShapeDtypeStruct((rows_to, cols_to), BF16),
        grid=(rows_to // bm,),
        in_specs=[pl.BlockSpec((bm, cols), lambda i: (jnp.minimum(i, row_blocks - 1), 0))],
        out_specs=pl.BlockSpec((bm, cols_to), lambda i: (i, 0)),
        compiler_params=_params(("parallel",), 2 * bm * (cols * 4 + cols_to * 2)),
        name="cast_pad",
    )(w)


def _round_up(n, mult):
    return -(-n // mult) * mult


def kernel(x_prompt, x_sample, norm_mix, norm_ffn, w_in_even, rpb_na, w_out_even, w_in_odd,
           lambda_q1, lambda_k1, lambda_q2, lambda_k2, subln_w, w_out_odd, t5_bias,
           w_gate, w_up, w_down, norm_final):
    d_model = x_prompt.shape[-1]
    depth = norm_mix.shape[0]
    na_heads = rpb_na.shape[1]
    na_width = na_heads * HEAD_DIM
    diff_heads = t5_bias.shape[1]
    diff_width = diff_heads * 2 * HEAD_DIM
    fnet_width = w_in_even.shape[-1] - 3 * na_width
    group_dim = fnet_width // FNET_GROUPS
    hidden = _round_up(w_gate.shape[-1], FFN_PAD)

    layers = []
    for layer in range(depth):
        lw = {}
        if layer % 2 == 0:
            e = layer // 2
            w_in = w_in_even[e].astype(BF16)
            lw.update(w_qkv=w_in[:, :3 * na_width], w_u=w_in[:, 3 * na_width:], w_out=w_out_even[e].astype(BF16),
                      rpb=rpb_na[e])
        else:
            o = layer // 2
            col_scale = jnp.where(jnp.arange(3 * diff_width) < diff_width, HEAD_DIM ** -0.5 * LOG2E, 1.0)
            lw.update(w_in=(w_in_odd[o] * col_scale.astype(F32)).astype(BF16), w_out=w_out_odd[o].astype(BF16),
                      bias_tiles=_t5_bias_tiles(t5_bias), subln_w=subln_w[o],
                      lam_vecs=[v[o].reshape(1, HEAD_DIM).astype(F32)
                                for v in (lambda_q1, lambda_k1, lambda_q2, lambda_k2)])
        lw.update(wg=cast_pad(w_gate[layer], d_model, hidden), wu=cast_pad(w_up[layer], d_model, hidden),
                  wd=cast_pad(w_down[layer], hidden, d_model))
        layers.append(lw)

    return tuple(_trunk(xb, layers, norm_mix, norm_ffn, norm_final, na_heads=na_heads, diff_heads=diff_heads,
                        group_dim=group_dim) for xb in (x_prompt, x_sample))


def _trunk(xb, layers, norm_mix, norm_ffn, norm_final, *, na_heads, diff_heads, group_dim):
    batch, seq, d_model = xb.shape
    diff_width = diff_heads * 2 * HEAD_DIM
    x = xb.reshape(batch * seq, d_model)
    for layer, lw in enumerate(layers):
        h = rmsnorm(x, norm_mix[layer], BF16)
        if layer % 2 == 0:
            qkv = matmul(h, lw["w_qkv"], BF16)
            u = matmul(h, lw["w_u"], BF16)
            kh = min(NA_ROWS, seq // GRID_W)
            oa = neighbourhood_attention(qkv, _na_bias_table(lw["rpb"], kh), row_start=0, batch=batch, seq=seq,
                                         heads=na_heads)
            ob = fourier_mix(u, row_start=0, batch=batch, seq=seq, group_dim=group_dim)
            x = matmul([oa, ob], lw["w_out"], F32, residual=x)
        else:
            proj = matmul(h, lw["w_in"], BF16)
            qt = jnp.transpose(proj[:, :diff_width].reshape(-1, diff_heads, 2 * HEAD_DIM), (1, 2, 0))
            vt = proj[:, 2 * diff_width:].reshape(-1, ATT_C, diff_heads, 2 * HEAD_DIM)
            vt = jnp.transpose(vt, (2, 0, 3, 1))
            vt = jnp.concatenate([vt, jnp.ones(vt.shape[:2] + (ATT_SUM_ROWS, ATT_C), BF16)], axis=2)
            lam_init = 0.8 - 0.6 * math.exp(-0.3 * layer)
            att = differential_attention(qt, proj, vt, lw["bias_tiles"], lw["lam_vecs"], lw["subln_w"], row_start=0,
                                         batch=batch, seq=seq, heads=diff_heads, lam_init=lam_init)
            x = matmul(att, lw["w_out"], F32, residual=x)
        h = rmsnorm(x, norm_ffn[layer], BF16)
        act = gate_up(h, lw["wg"], lw["wu"])
        x = matmul(act, lw["wd"], F32, residual=x, bn=2048, tk=1024)
    return rmsnorm(x, norm_final, F32).reshape(xb.shape)
```

```python
import functools
import math

import jax
import jax.numpy as jnp
from jax import lax
from jax.experimental import pallas as pl
from jax.experimental.pallas import tpu as pltpu

F32 = jnp.float32
BF16 = jnp.bfloat16

HEAD_DIM = 128
GRID_W = 64
NA_ROWS = 8
NA_COLS = 16
FNET_GROUPS = 4
T5_BUCKETS = 32
T5_MAX_DIST = 128
RMS_EPS = 1e-6
SUBLN_EPS = 1e-5
NEG_BIG = -1e30

V7X_VMEM_LIMIT = 56 * 1024 * 1024
ATT_R = 256
ATT_C = 256
ATT_TQ = 512
ATT_UNROLL = 4
NA_UNROLL = 8
FFN_PAD = 1024
LOG2E = math.log2(math.e)


def _toeplitz(w, rows, cols):
    lead = w.shape[:-1]
    period = rows + cols
    wp = jnp.concatenate([w, jnp.zeros(lead + (1,), w.dtype)], axis=-1)
    flat = jnp.tile(wp, (1,) * len(lead) + (rows,))[..., :rows * (period - 1)]
    return flat.reshape(lead + (rows, period - 1))[..., rows - 1:rows - 1 + cols]


def _params(sem, vmem_bytes):
    return pltpu.CompilerParams(dimension_semantics=sem,
                                vmem_limit_bytes=int(min(V7X_VMEM_LIMIT, vmem_bytes * 3 // 2)))


def _rmsnorm_kernel(x_ref, g_ref, o_ref, *, eps):
    x = x_ref[...]
    inv = lax.rsqrt(jnp.mean(x * x, axis=-1, keepdims=True) + eps)
    o_ref[...] = (x * inv * g_ref[...]).astype(o_ref.dtype)


def rmsnorm(x, g, out_dtype, *, row_start=0, rows=None, bm=256):
    total, d = x.shape
    rows = total if rows is None else rows
    off = row_start // bm
    assert row_start % bm == 0 and rows % bm == 0
    return pl.pallas_call(
        functools.partial(_rmsnorm_kernel, eps=RMS_EPS),
        out_shape=jax.ShapeDtypeStruct((rows, d), out_dtype),
        grid=(rows // bm,),
        in_specs=[pl.BlockSpec((bm, d), lambda i: (i + off, 0)),
                  pl.BlockSpec((1, d), lambda i: (0, 0))],
        out_specs=pl.BlockSpec((bm, d), lambda i: (i, 0)),
        compiler_params=_params(("parallel",), 6 * bm * d * 4),
        name="rmsnorm",
    )(x, g.reshape(1, d).astype(F32))


def _matmul_kernel(*refs, nk, na, has_res):
    a_refs, w_ref = refs[:na], refs[na]
    r_ref = refs[na + 1] if has_res else None
    o_ref = refs[-1]

    def part():
        acc, start = None, 0
        for a_ref in a_refs:
            stop = start + a_ref.shape[1]
            d = jnp.dot(a_ref[...], w_ref[start:stop, :], preferred_element_type=F32)
            acc = d if acc is None else acc + d
            start = stop
        return acc

    if nk == 1:
        o_ref[...] = (part() + r_ref[...] if r_ref is not None else part()).astype(o_ref.dtype)
        return
    k = pl.program_id(2)

    @pl.when(k == 0)
    def _():
        o_ref[...] = part() + r_ref[...] if r_ref is not None else part()

    @pl.when(k > 0)
    def _():
        o_ref[...] += part()


def matmul(a, w, out_dtype, *, residual=None, bm=1024, bn=1024, tk=None):
    a_list = list(a) if isinstance(a, (list, tuple)) else [a]
    m = a_list[0].shape[0]
    kdim, n = w.shape
    assert sum(x.shape[1] for x in a_list) == kdim
    tk = kdim if tk is None else tk
    bm, bn = min(bm, m), min(bn, n)
    assert m % bm == 0 and n % bn == 0 and kdim % tk == 0
    nk = kdim // tk
    assert nk == 1 or len(a_list) == 1
    has_res = residual is not None
    in_specs = [pl.BlockSpec((bm, x.shape[1] // nk), lambda i, j, k: (i, k)) for x in a_list]
    in_specs.append(pl.BlockSpec((tk, bn), lambda i, j, k: (k, j)))
    args = a_list + [w]
    if has_res:
        in_specs.append(pl.BlockSpec((bm, bn), lambda i, j, k: (i, j)))
        args.append(residual)
    assert nk == 1 or out_dtype == F32
    out_bytes = jnp.dtype(out_dtype).itemsize
    vmem = (2 * (bm * tk + tk * bn) * 2 + 2 * bm * bn * out_bytes + (2 * bm * bn * 4 if has_res else 0)
            + bm * bn * 4)
    return pl.pallas_call(
        functools.partial(_matmul_kernel, nk=nk, na=len(a_list), has_res=has_res),
        out_shape=jax.ShapeDtypeStruct((m, n), out_dtype),
        grid=(m // bm, n // bn, nk),
        in_specs=in_specs,
        out_specs=pl.BlockSpec((bm, bn), lambda i, j, k: (i, j)),
        compiler_params=_params(("parallel", "parallel", "arbitrary"), vmem),
        name="matmul",
    )(*args)


def _gate_up_kernel(a_ref, wg_ref, wu_ref, o_ref):
    a = a_ref[...]
    g = jnp.dot(a, wg_ref[...], preferred_element_type=F32)
    u = jnp.dot(a, wu_ref[...], preferred_element_type=F32)
    o_ref[...] = (g * jax.nn.sigmoid(g) * u).astype(o_ref.dtype)


def gate_up(h, wg, wu, *, bm=1024, bn=512):
    m, kdim = h.shape
    _, n = wg.shape
    assert m % bm == 0 and n % bn == 0
    vmem = 2 * bm * kdim * 2 + 4 * kdim * bn * 2 + 2 * bm * bn * 2 + 4 * bm * bn * 4
    return pl.pallas_call(
        _gate_up_kernel,
        out_shape=jax.ShapeDtypeStruct((m, n), BF16),
        grid=(m // bm, n // bn),
        in_specs=[pl.BlockSpec((bm, kdim), lambda i, j: (i, 0)),
                  pl.BlockSpec((kdim, bn), lambda i, j: (0, j)),
                  pl.BlockSpec((kdim, bn), lambda i, j: (0, j))],
        out_specs=pl.BlockSpec((bm, bn), lambda i, j: (i, j)),
        compiler_params=_params(("parallel", "parallel"), vmem),
        name="gate_up",
    )(h, wg, wu)


def _na_kernel(q_ref, k_ref, v_ref, bias_ref, o_ref, *, rows, kh):
    scale = HEAD_DIM ** -0.5
    win = kh * GRID_W

    def row_group(g, carry):
        rws = [g * NA_UNROLL + t for t in range(NA_UNROLL)]
        starts = [jnp.clip(r - kh // 2, 0, rows - kh) for r in rws]
        qoffs = [pl.multiple_of(r * GRID_W, GRID_W) for r in rws]
        koffs = [pl.multiple_of(rs * GRID_W, GRID_W) for rs in starts]
        ss = [lax.dot_general(q_ref[pl.ds(qo, GRID_W), :], k_ref[pl.ds(ko, win), :], (((1,), (1,)), ((), ())),
                              preferred_element_type=F32) for qo, ko in zip(qoffs, koffs)]
        ps, ls = [], []
        for s, r, rs in zip(ss, rws, starts):
            s = s * scale + bias_ref[r - rs]
            p = jnp.exp(s - jnp.max(s, axis=-1, keepdims=True))
            ls.append(jnp.sum(p, axis=-1, keepdims=True))
            ps.append(p.astype(v_ref.dtype))
        os_ = [jnp.dot(p, v_ref[pl.ds(ko, win), :], preferred_element_type=F32) for p, ko in zip(ps, koffs)]
        for o, l, qo in zip(os_, ls, qoffs):
            o_ref[pl.ds(qo, GRID_W), :] = (o / l).astype(o_ref.dtype)
        return carry

    assert rows % NA_UNROLL == 0
    lax.fori_loop(0, rows // NA_UNROLL, row_group, 0)


def _na_bias_table(rpb, kh):
    heads = rpb.shape[0]
    c = jnp.arange(GRID_W, dtype=jnp.int32)
    cs = jnp.clip(c - NA_COLS // 2, 0, GRID_W - NA_COLS)
    j = jnp.arange(GRID_W, dtype=jnp.int32)
    valid = (j[None, :] >= cs[:, None]) & (j[None, :] < cs[:, None] + NA_COLS)
    pad = GRID_W - NA_COLS
    w = jnp.pad(rpb.astype(F32), ((0, 0), (0, 0), (pad, pad)))
    t = jnp.where(valid, _toeplitz(w, GRID_W, GRID_W), NEG_BIG)
    t = jnp.transpose(t, (0, 2, 1, 3))
    tabs = [t[:, :, NA_ROWS - 1 - dl:NA_ROWS - 1 - dl + kh, :].reshape(heads, GRID_W, kh * GRID_W)
            for dl in range(kh)]
    return jnp.stack(tabs, axis=1)


def neighbourhood_attention(qkv, bias_tab, *, row_start, batch, seq, heads):
    rows = seq // GRID_W
    kh = bias_tab.shape[1]
    off = row_start // seq
    assert row_start % seq == 0
    vmem = 2 * 4 * seq * HEAD_DIM * 2 + 2 * bias_tab[0].size * 4 + (8 << 20)
    return pl.pallas_call(
        functools.partial(_na_kernel, rows=rows, kh=kh),
        out_shape=jax.ShapeDtypeStruct((batch * seq, heads * HEAD_DIM), BF16),
        grid=(batch, heads),
        in_specs=[pl.BlockSpec((seq, HEAD_DIM), lambda b, h: (b + off, h)),
                  pl.BlockSpec((seq, HEAD_DIM), lambda b, h: (b + off, heads + h)),
                  pl.BlockSpec((seq, HEAD_DIM), lambda b, h: (b + off, 2 * heads + h)),
                  pl.BlockSpec((None, kh, GRID_W, kh * GRID_W), lambda b, h: (h, 0, 0, 0))],
        out_specs=pl.BlockSpec((seq, HEAD_DIM), lambda b, h: (b, h)),
        compiler_params=_params(("parallel", "parallel"), vmem),
        name="neighbourhood_attention",
    )(qkv, qkv, qkv, bias_tab)


DFT_N2 = 64
DFT_CB = 8
LANES = 128


def _cos_sin(num, den):
    ang = (num % den).astype(F32) * (2.0 * math.pi / den)
    return jnp.cos(ang), jnp.sin(ang)


def _dft_stage1_kernel(w_ref, x_ref, o_ref):
    o_ref[...] = jnp.dot(w_ref[...], x_ref[...], preferred_element_type=F32).astype(o_ref.dtype)


def _dft_stage2_kernel(tr_ref, ti_ref, tc_ref, ts_ref, w2_ref, wg_ref, o_ref, *, groups, gdim):
    cb, n2, width = tr_ref.shape
    rep = width // LANES
    xr, xi = [], []
    for c in range(cb):
        tr = tr_ref[c].astype(F32)
        ti = ti_ref[c].astype(F32)
        tc = jnp.tile(tc_ref[c], (1, rep))
        ts = jnp.tile(ts_ref[c], (1, rep))
        stacked = jnp.concatenate([tr * tc + ti * ts, ti * tc - tr * ts], axis=0).astype(BF16)
        x = jnp.dot(w2_ref[...], stacked, preferred_element_type=F32)
        xr.append(x[:n2])
        xi.append(x[n2:])
    xr = jnp.concatenate(xr, axis=0).astype(BF16)
    xi = jnp.concatenate(xi, axis=0).astype(BF16)
    for g in range(groups):
        cols = slice(g * gdim, (g + 1) * gdim)
        f = (jnp.dot(xr[:, cols], wg_ref[0], preferred_element_type=F32)
             + jnp.dot(xi[:, cols], wg_ref[1], preferred_element_type=F32))
        for c in range(cb):
            o_ref[:, c * width + g * gdim:c * width + (g + 1) * gdim] = f[c * n2:(c + 1) * n2].astype(o_ref.dtype)


def fourier_mix(u, *, row_start, batch, seq, group_dim):
    total, width = u.shape
    groups = width // group_dim
    n2 = DFT_N2
    n1 = seq // n2
    cb = min(DFT_CB, n1)
    assert seq % n2 == 0 and n1 % cb == 0 and row_start % seq == 0 and total % n2 == 0
    row = n2 * width
    bn = min(row, 8192)
    a = jnp.arange(n1, dtype=jnp.int32)
    b = jnp.arange(n2, dtype=jnp.int32)
    c1, s1 = _cos_sin(a[:, None] * a[None, :], n1)
    w1 = (jnp.concatenate([c1, -s1], axis=0) * n1 ** -0.5).astype(BF16)
    off = row_start // seq
    t = pl.pallas_call(
        _dft_stage1_kernel,
        out_shape=jax.ShapeDtypeStruct((batch, 2 * n1, row), BF16),
        grid=(batch, row // bn),
        in_specs=[pl.BlockSpec((2 * n1, n1), lambda bi, j: (0, 0)),
                  pl.BlockSpec((n1, bn), lambda bi, j: (off + bi, j))],
        out_specs=pl.BlockSpec((None, 2 * n1, bn), lambda bi, j: (bi, 0, j)),
        compiler_params=_params(("parallel", "parallel"), 2 * (3 * n1 * bn * 2 + 2 * n1 * n1 * 2) + 2 * n1 * bn * 4),
        name="dft_stage1",
    )(w1, u.reshape(total // n2, row))
    t = t.reshape(batch, 2 * n1, n2, width)
    tc, ts = _cos_sin(a[:, None] * b[None, :], seq)
    tc = jnp.broadcast_to(tc[:, :, None], (n1, n2, LANES))
    ts = jnp.broadcast_to(ts[:, :, None], (n1, n2, LANES))
    c2, s2 = _cos_sin(b[:, None] * b[None, :], n2)
    w2 = (jnp.block([[c2, s2], [-s2, c2]]) * n2 ** -0.5).astype(BF16)
    gi = jnp.arange(group_dim, dtype=jnp.int32)
    cg, sg = _cos_sin(gi[:, None] * gi[None, :], group_dim)
    wg = (jnp.stack([cg, sg]) * group_dim ** -0.5).astype(BF16)
    nj = n1 // cb
    vmem = (2 * (2 * cb * n2 * width * 2 + 2 * cb * n2 * LANES * 4 + n2 * cb * width * 2) + 4 * n2 * n2 * 2
            + 2 * wg.size * 2 + 6 * cb * n2 * width * 4)
    out = pl.pallas_call(
        functools.partial(_dft_stage2_kernel, groups=groups, gdim=group_dim),
        out_shape=jax.ShapeDtypeStruct((batch, n2, n1 * width), BF16),
        grid=(batch, nj),
        in_specs=[pl.BlockSpec((None, cb, n2, width), lambda bi, j: (bi, j, 0, 0)),
                  pl.BlockSpec((None, cb, n2, width), lambda bi, j: (bi, nj + j, 0, 0)),
                  pl.BlockSpec((cb, n2, LANES), lambda bi, j: (j, 0, 0)),
                  pl.BlockSpec((cb, n2, LANES), lambda bi, j: (j, 0, 0)),
                  pl.BlockSpec((2 * n2, 2 * n2), lambda bi, j: (0, 0)),
                  pl.BlockSpec((2, group_dim, group_dim), lambda bi, j: (0, 0, 0))],
        out_specs=pl.BlockSpec((None, n2, cb * width), lambda bi, j: (bi, 0, j)),
        compiler_params=_params(("parallel", "parallel"), vmem),
        name="dft_stage2",
    )(t, t, tc, ts, w2, wg)
    return out.reshape(batch * seq, width)


def _t5_bucket(rel):
    nb = T5_BUCKETS // 2
    max_exact = nb // 2
    ret = (rel > 0).astype(jnp.int32) * nb
    n = jnp.abs(rel)
    nf = jnp.maximum(n, 1).astype(F32)
    large = max_exact + (jnp.log(nf / max_exact) / math.log(T5_MAX_DIST / max_exact)
                         * (nb - max_exact)).astype(jnp.int32)
    large = jnp.minimum(large, nb - 1)
    return ret + jnp.where(n < max_exact, n, large)


ATT_BIAS_CHUNKS = (-2, -1, 0, 1, 2)


def _t5_bias_tiles(t5_bias):
    assert ATT_C == ATT_R and ATT_R + 1 >= T5_MAX_DIST
    tiles = []
    for dc in ATT_BIAS_CHUNKS:
        rel = dc * ATT_C + ATT_C - 1 - jnp.arange(ATT_C + ATT_R - 1, dtype=jnp.int32)
        w = t5_bias.astype(F32)[_t5_bucket(rel)].T * LOG2E
        tiles.append(_toeplitz(w, ATT_C, ATT_R))
    return jnp.stack(tiles, axis=1)


ATT_SUM_ROWS = 16

def _diff_attn_kernel(qt_ref, k_ref, vt_ref, nb_ref, lq1_ref, lk1_ref, lq2_ref, lk2_ref, w_ref, o_ref,
                      m_ref, a_ref, s_ref, p_ref, al_ref, *, nc, nsub, lam_init):
    i = pl.program_id(2)
    d = HEAD_DIM
    m_ref[...] = jnp.full(m_ref.shape, NEG_BIG, F32)
    a_ref[...] = jnp.zeros(a_ref.shape, F32)
    lo, hi = ATT_BIAS_CHUNKS[0], ATT_BIAS_CHUNKS[-1]
    units = [(r, half) for r in range(nsub) for half in (0, 1)]

    def scores(c, slot):
        for r, half in units:
            kc = k_ref[pl.ds(pl.multiple_of(c * ATT_C, ATT_C), ATT_C), half * d:(half + 1) * d]
            qt = qt_ref[half * d:(half + 1) * d, r * ATT_R:(r + 1) * ATT_R]
            s_ref[slot, r, half] = jnp.dot(kc, qt, preferred_element_type=F32)

    def softmax(c, slot):
        for r, half in units:
            tile = jnp.clip(c - (i * nsub + r), lo, hi) - lo
            for q0 in range(0, ATT_R, LANES):
                cols = slice(q0, q0 + LANES)
                s = s_ref[slot, r, half, :, cols] + nb_ref[tile, :, cols]
                m_old = m_ref[r, half, :, cols]
                m_new = jnp.maximum(m_old, jnp.max(s, axis=0, keepdims=True))
                m_ref[r, half, :, cols] = m_new
                al_ref[slot, r, half, :, cols] = jnp.exp2(m_old - m_new)
                p_ref[slot, r, half, :, cols] = jnp.exp2(s - m_new).astype(p_ref.dtype)

    def accumulate(c, slot):
        for r, half in units:
            a_ref[r, half] = (al_ref[slot, r, half] * a_ref[r, half]
                              + jnp.dot(vt_ref[c], p_ref[slot, r, half], preferred_element_type=F32))

    def advance(c, t):
        accumulate(c - 2, t)
        softmax(c - 1, 1 - t)
        scores(c, t)

    scores(0, 0)
    scores(1, 1)
    softmax(0, 0)
    for c in range(2, ATT_UNROLL):
        advance(c, c % 2)

    def step(j, carry):
        for t in range(ATT_UNROLL):
            advance(ATT_UNROLL * j + t, t % 2)
        return carry

    lax.fori_loop(1, nc // ATT_UNROLL, step, 0)
    accumulate(nc - 2, 0)
    softmax(nc - 1, 1)
    accumulate(nc - 1, 1)

    lam = (jnp.exp(jnp.sum(lq1_ref[...] * lk1_ref[...], axis=-1, keepdims=True))
           - jnp.exp(jnp.sum(lq2_ref[...] * lk2_ref[...], axis=-1, keepdims=True)) + lam_init)
    hd2 = 2 * d
    for r in range(nsub):
        o1, o2 = (a_ref[r, half, :hd2] / a_ref[r, half, hd2:hd2 + 1] for half in (0, 1))
        att = o1 - lam * o2
        inv = lax.rsqrt(jnp.mean(att * att, axis=0, keepdims=True) + SUBLN_EPS)
        y = (att * inv).T * w_ref[...] * (1.0 - lam_init)
        o_ref[r * ATT_R:(r + 1) * ATT_R, :] = y.astype(o_ref.dtype)


def differential_attention(qt, proj, vt, bias_tiles, lam_vecs, subln_w, *, row_start, batch, seq, heads, lam_init):
    tq = min(ATT_TQ, seq)
    nq = seq // tq
    nc = seq // ATT_C
    nsub = tq // ATT_R
    assert seq % tq == 0 and row_start % seq == 0 and tq % ATT_R == 0 and nc % ATT_UNROLL == 0
    qoff = row_start // tq
    soff = row_start // seq
    hd2 = 2 * HEAD_DIM
    rows_a = hd2 + ATT_SUM_ROWS
    vec = pl.BlockSpec((1, HEAD_DIM), lambda b, h, i: (0, 0))
    vmem = (2 * seq * (hd2 + rows_a) * 2 + 2 * bias_tiles[0].size * 4 + 4 * tq * hd2 * 2
            + 2 * nsub * ATT_R * ((rows_a + 3 * 8) * 4 + 2 * ATT_C * 6))
    return pl.pallas_call(
        functools.partial(_diff_attn_kernel, nc=nc, nsub=nsub, lam_init=lam_init),
        out_shape=jax.ShapeDtypeStruct((batch * seq, heads * hd2), BF16),
        grid=(batch, heads, nq),
        in_specs=[pl.BlockSpec((None, hd2, tq), lambda b, h, i: (h, 0, qoff + b * nq + i)),
                  pl.BlockSpec((seq, hd2), lambda b, h, i: (soff + b, heads + h)),
                  pl.BlockSpec((None, nc, rows_a, ATT_C), lambda b, h, i: (h, soff + b, 0, 0)),
                  pl.BlockSpec((None,) + bias_tiles.shape[1:], lambda b, h, i: (h, 0, 0, 0)),
                  vec, vec, vec, vec,
                  pl.BlockSpec((1, hd2), lambda b, h, i: (0, 0))],
        out_specs=pl.BlockSpec((tq, hd2), lambda b, h, i: (b * nq + i, h)),
        scratch_shapes=[pltpu.VMEM((nsub, 2, 1, ATT_R), F32), pltpu.VMEM((nsub, 2, rows_a, ATT_R), F32),
                        pltpu.VMEM((2, nsub, 2, ATT_C, ATT_R), F32), pltpu.VMEM((2, nsub, 2, ATT_C, ATT_R), BF16),
                        pltpu.VMEM((2, nsub, 2, 1, ATT_R), F32)],
        compiler_params=_params(("parallel", "parallel", "arbitrary"), vmem),
        name="differential_attention",
    )(qt, proj, vt, bias_tiles, *lam_vecs, subln_w.reshape(1, hd2).astype(F32))


def _cast_pad_kernel(x_ref, o_ref, *, row_blocks, pad_rows):
    cols = x_ref.shape[1]

    def copy():
        o_ref[:, :cols] = x_ref[...].astype(o_ref.dtype)
        if cols < o_ref.shape[1]:
            o_ref[:, cols:] = jnp.zeros((o_ref.shape[0], o_ref.shape[1] - cols), o_ref.dtype)

    if not pad_rows:
        copy()
    else:
        pl.when(pl.program_id(0) < row_blocks)(copy)

        @pl.when(pl.program_id(0) >= row_blocks)
        def _():
            o_ref[...] = jnp.zeros(o_ref.shape, o_ref.dtype)


def cast_pad(w, layer, rows_to, cols_to, *, bm=LANES):
    _, rows, cols = w.shape
    assert rows % bm == 0 and rows_to % bm == 0 and cols % LANES == 0 and cols_to % LANES == 0
    row_blocks = rows // bm
    return pl.pallas_call(
        functools.partial(_cast_pad_kernel, row_blocks=row_blocks, pad_rows=rows_to > rows),
        out_shape=jax.ShapeDtypeStruct((rows_to, cols_to), BF16),
        grid=(rows_to // bm,),
        in_specs=[pl.BlockSpec((None, bm, cols), lambda i: (layer, jnp.minimum(i, row_blocks - 1), 0))],
        out_specs=pl.BlockSpec((bm, cols_to), lambda i: (i, 0)),
        compiler_params=_params(("parallel",), 2 * bm * (cols * 4 + cols_to * 2)),
        name="cast_pad",
    )(w)


def _round_up(n, mult):
    return -(-n // mult) * mult


def kernel(x_prompt, x_sample, norm_mix, norm_ffn, w_in_even, rpb_na, w_out_even, w_in_odd,
           lambda_q1, lambda_k1, lambda_q2, lambda_k2, subln_w, w_out_odd, t5_bias,
           w_gate, w_up, w_down, norm_final):
    d_model = x_prompt.shape[-1]
    depth = norm_mix.shape[0]
    na_heads = rpb_na.shape[1]
    na_width = na_heads * HEAD_DIM
    diff_heads = t5_bias.shape[1]
    diff_width = diff_heads * 2 * HEAD_DIM
    fnet_width = w_in_even.shape[-1] - 3 * na_width
    group_dim = fnet_width // FNET_GROUPS
    hidden = _round_up(w_gate.shape[-1], FFN_PAD)

    layers = []
    for layer in range(depth):
        lw = {}
        if layer % 2 == 0:
            e = layer // 2
            w_in = w_in_even[e].astype(BF16)
            lw.update(w_qkv=w_in[:, :3 * na_width], w_u=w_in[:, 3 * na_width:], w_out=w_out_even[e].astype(BF16),
                      rpb=rpb_na[e])
        else:
            o = layer // 2
            col_scale = jnp.where(jnp.arange(3 * diff_width) < diff_width, HEAD_DIM ** -0.5 * LOG2E, 1.0)
            lw.update(w_in=(w_in_odd[o] * col_scale.astype(F32)).astype(BF16), w_out=w_out_odd[o].astype(BF16),
                      bias_tiles=_t5_bias_tiles(t5_bias), subln_w=subln_w[o],
                      lam_vecs=[v[o].reshape(1, HEAD_DIM).astype(F32)
                                for v in (lambda_q1, lambda_k1, lambda_q2, lambda_k2)])
        lw.update(wg=cast_pad(w_gate, layer, d_model, hidden), wu=cast_pad(w_up, layer, d_model, hidden),
                  wd=cast_pad(w_down, layer, hidden, d_model))
        layers.append(lw)

    return tuple(_trunk(xb, layers, norm_mix, norm_ffn, norm_final, na_heads=na_heads, diff_heads=diff_heads,
                        group_dim=group_dim) for xb in (x_prompt, x_sample))


def _trunk(xb, layers, norm_mix, norm_ffn, norm_final, *, na_heads, diff_heads, group_dim):
    batch, seq, d_model = xb.shape
    diff_width = diff_heads * 2 * HEAD_DIM
    x = xb.reshape(batch * seq, d_model)
    for layer, lw in enumerate(layers):
        h = rmsnorm(x, norm_mix[layer], BF16)
        if layer % 2 == 0:
            qkv = matmul(h, lw["w_qkv"], BF16)
            u = matmul(h, lw["w_u"], BF16)
            kh = min(NA_ROWS, seq // GRID_W)
            oa = neighbourhood_attention(qkv, _na_bias_table(lw["rpb"], kh), row_start=0, batch=batch, seq=seq,
                                         heads=na_heads)
            ob = fourier_mix(u, row_start=0, batch=batch, seq=seq, group_dim=group_dim)
            x = matmul([oa, ob], lw["w_out"], F32, residual=x)
        else:
            proj = matmul(h, lw["w_in"], BF16)
            qt = jnp.transpose(proj[:, :diff_width].reshape(-1, diff_heads, 2 * HEAD_DIM), (1, 2, 0))
            vt = proj[:, 2 * diff_width:].reshape(-1, ATT_C, diff_heads, 2 * HEAD_DIM)
            vt = jnp.transpose(vt, (2, 0, 3, 1))
            vt = jnp.concatenate([vt, jnp.ones(vt.shape[:2] + (ATT_SUM_ROWS, ATT_C), BF16)], axis=2)
            lam_init = 0.8 - 0.6 * math.exp(-0.3 * layer)
            att = differential_attention(qt, proj, vt, lw["bias_tiles"], lw["lam_vecs"], lw["subln_w"], row_start=0,
                                         batch=batch, seq=seq, heads=diff_heads, lam_init=lam_init)
            x = matmul(att, lw["w_out"], F32, residual=x)
        h = rmsnorm(x, norm_ffn[layer], BF16)
        act = gate_up(h, lw["wg"], lw["wu"])
        x = matmul(act, lw["wd"], F32, residual=x, bn=2048, tk=1024)
    return rmsnorm(x, norm_final, F32).reshape(xb.shape)
```

```python
import functools
import math

import jax
import jax.numpy as jnp
from jax import lax
from jax.experimental import pallas as pl
from jax.experimental.pallas import tpu as pltpu

F32 = jnp.float32
BF16 = jnp.bfloat16

HEAD_DIM = 128
GRID_W = 64
NA_ROWS = 8
NA_COLS = 16
FNET_GROUPS = 4
T5_BUCKETS = 32
T5_MAX_DIST = 128
RMS_EPS = 1e-6
SUBLN_EPS = 1e-5
NEG_BIG = -1e30

V7X_VMEM_LIMIT = 56 * 1024 * 1024
ATT_R = 256
ATT_C = 256
ATT_TQ = 512
ATT_UNROLL = 4
NA_UNROLL = 8
FFN_PAD = 1024
LOG2E = math.log2(math.e)


def _toeplitz(w, rows, cols):
    lead = w.shape[:-1]
    period = rows + cols
    wp = jnp.concatenate([w, jnp.zeros(lead + (1,), w.dtype)], axis=-1)
    flat = jnp.tile(wp, (1,) * len(lead) + (rows,))[..., :rows * (period - 1)]
    return flat.reshape(lead + (rows, period - 1))[..., rows - 1:rows - 1 + cols]


def _params(sem, vmem_bytes):
    return pltpu.CompilerParams(dimension_semantics=sem,
                                vmem_limit_bytes=int(min(V7X_VMEM_LIMIT, vmem_bytes * 3 // 2)))


def _rmsnorm_kernel(x_ref, g_ref, o_ref, *, eps):
    x = x_ref[...]
    inv = lax.rsqrt(jnp.mean(x * x, axis=-1, keepdims=True) + eps)
    o_ref[...] = (x * inv * g_ref[...]).astype(o_ref.dtype)


def rmsnorm(x, g, out_dtype, *, row_start=0, rows=None, bm=256):
    total, d = x.shape
    rows = total if rows is None else rows
    off = row_start // bm
    assert row_start % bm == 0 and rows % bm == 0
    return pl.pallas_call(
        functools.partial(_rmsnorm_kernel, eps=RMS_EPS),
        out_shape=jax.ShapeDtypeStruct((rows, d), out_dtype),
        grid=(rows // bm,),
        in_specs=[pl.BlockSpec((bm, d), lambda i: (i + off, 0)),
                  pl.BlockSpec((1, d), lambda i: (0, 0))],
        out_specs=pl.BlockSpec((bm, d), lambda i: (i, 0)),
        compiler_params=_params(("parallel",), 6 * bm * d * 4),
        name="rmsnorm",
    )(x, g.reshape(1, d).astype(F32))


def _matmul_kernel(*refs, nk, na, has_res):
    a_refs, w_ref = refs[:na], refs[na]
    r_ref = refs[na + 1] if has_res else None
    o_ref = refs[-1]

    def part():
        acc, start = None, 0
        for a_ref in a_refs:
            stop = start + a_ref.shape[1]
            d = jnp.dot(a_ref[...], w_ref[start:stop, :], preferred_element_type=F32)
            acc = d if acc is None else acc + d
            start = stop
        return acc

    if nk == 1:
        o_ref[...] = (part() + r_ref[...] if r_ref is not None else part()).astype(o_ref.dtype)
        return
    k = pl.program_id(2)

    @pl.when(k == 0)
    def _():
        o_ref[...] = part() + r_ref[...] if r_ref is not None else part()

    @pl.when(k > 0)
    def _():
        o_ref[...] += part()


def matmul(a, w, out_dtype, *, residual=None, bm=1024, bn=1024, tk=None):
    a_list = list(a) if isinstance(a, (list, tuple)) else [a]
    m = a_list[0].shape[0]
    kdim, n = w.shape
    assert sum(x.shape[1] for x in a_list) == kdim
    tk = kdim if tk is None else tk
    bm, bn = min(bm, m), min(bn, n)
    assert m % bm == 0 and n % bn == 0 and kdim % tk == 0
    nk = kdim // tk
    assert nk == 1 or len(a_list) == 1
    has_res = residual is not None
    in_specs = [pl.BlockSpec((bm, x.shape[1] // nk), lambda i, j, k: (i, k)) for x in a_list]
    in_specs.append(pl.BlockSpec((tk, bn), lambda i, j, k: (k, j)))
    args = a_list + [w]
    if has_res:
        in_specs.append(pl.BlockSpec((bm, bn), lambda i, j, k: (i, j)))
        args.append(residual)
    assert nk == 1 or out_dtype == F32
    out_bytes = jnp.dtype(out_dtype).itemsize
    vmem = (2 * (bm * tk + tk * bn) * 2 + 2 * bm * bn * out_bytes + (2 * bm * bn * 4 if has_res else 0)
            + bm * bn * 4)
    return pl.pallas_call(
        functools.partial(_matmul_kernel, nk=nk, na=len(a_list), has_res=has_res),
        out_shape=jax.ShapeDtypeStruct((m, n), out_dtype),
        grid=(m // bm, n // bn, nk),
        in_specs=in_specs,
        out_specs=pl.BlockSpec((bm, bn), lambda i, j, k: (i, j)),
        compiler_params=_params(("parallel", "parallel", "arbitrary"), vmem),
        name="matmul",
    )(*args)


def _gate_up_kernel(a_ref, wg_ref, wu_ref, o_ref):
    a = a_ref[...]
    g = jnp.dot(a, wg_ref[...], preferred_element_type=F32)
    u = jnp.dot(a, wu_ref[...], preferred_element_type=F32)
    o_ref[...] = (g * jax.nn.sigmoid(g) * u).astype(o_ref.dtype)


def gate_up(h, wg, wu, *, bm=1024, bn=512):
    m, kdim = h.shape
    _, n = wg.shape
    assert m % bm == 0 and n % bn == 0
    vmem = 2 * bm * kdim * 2 + 4 * kdim * bn * 2 + 2 * bm * bn * 2 + 4 * bm * bn * 4
    return pl.pallas_call(
        _gate_up_kernel,
        out_shape=jax.ShapeDtypeStruct((m, n), BF16),
        grid=(m // bm, n // bn),
        in_specs=[pl.BlockSpec((bm, kdim), lambda i, j: (i, 0)),
                  pl.BlockSpec((kdim, bn), lambda i, j: (0, j)),
                  pl.BlockSpec((kdim, bn), lambda i, j: (0, j))],
        out_specs=pl.BlockSpec((bm, bn), lambda i, j: (i, j)),
        compiler_params=_params(("parallel", "parallel"), vmem),
        name="gate_up",
    )(h, wg, wu)


def _na_kernel(q_ref, k_ref, v_ref, bias_ref, o_ref, *, rows, kh):
    scale = HEAD_DIM ** -0.5
    win = kh * GRID_W

    def row_group(g, carry):
        rws = [g * NA_UNROLL + t for t in range(NA_UNROLL)]
        starts = [jnp.clip(r - kh // 2, 0, rows - kh) for r in rws]
        qoffs = [pl.multiple_of(r * GRID_W, GRID_W) for r in rws]
        koffs = [pl.multiple_of(rs * GRID_W, GRID_W) for rs in starts]
        ss = [lax.dot_general(q_ref[pl.ds(qo, GRID_W), :], k_ref[pl.ds(ko, win), :], (((1,), (1,)), ((), ())),
                              preferred_element_type=F32) for qo, ko in zip(qoffs, koffs)]
        ps, ls = [], []
        for s, r, rs in zip(ss, rws, starts):
            s = s * scale + bias_ref[r - rs]
            p = jnp.exp(s - jnp.max(s, axis=-1, keepdims=True))
            ls.append(jnp.sum(p, axis=-1, keepdims=True))
            ps.append(p.astype(v_ref.dtype))
        os_ = [jnp.dot(p, v_ref[pl.ds(ko, win), :], preferred_element_type=F32) for p, ko in zip(ps, koffs)]
        for o, l, qo in zip(os_, ls, qoffs):
            o_ref[pl.ds(qo, GRID_W), :] = (o / l).astype(o_ref.dtype)
        return carry

    assert rows % NA_UNROLL == 0
    lax.fori_loop(0, rows // NA_UNROLL, row_group, 0)


def _na_bias_table(rpb, kh):
    heads = rpb.shape[0]
    c = jnp.arange(GRID_W, dtype=jnp.int32)
    cs = jnp.clip(c - NA_COLS // 2, 0, GRID_W - NA_COLS)
    j = jnp.arange(GRID_W, dtype=jnp.int32)
    valid = (j[None, :] >= cs[:, None]) & (j[None, :] < cs[:, None] + NA_COLS)
    pad = GRID_W - NA_COLS
    w = jnp.pad(rpb.astype(F32), ((0, 0), (0, 0), (pad, pad)))
    t = jnp.where(valid, _toeplitz(w, GRID_W, GRID_W), NEG_BIG)
    t = jnp.transpose(t, (0, 2, 1, 3))
    tabs = [t[:, :, NA_ROWS - 1 - dl:NA_ROWS - 1 - dl + kh, :].reshape(heads, GRID_W, kh * GRID_W)
            for dl in range(kh)]
    return jnp.stack(tabs, axis=1)


def neighbourhood_attention(qkv, bias_tab, *, row_start, batch, seq, heads):
    rows = seq // GRID_W
    kh = bias_tab.shape[1]
    off = row_start // seq
    assert row_start % seq == 0
    vmem = 2 * 4 * seq * HEAD_DIM * 2 + 2 * bias_tab[0].size * 4 + (8 << 20)
    return pl.pallas_call(
        functools.partial(_na_kernel, rows=rows, kh=kh),
        out_shape=jax.ShapeDtypeStruct((batch * seq, heads * HEAD_DIM), BF16),
        grid=(batch, heads),
        in_specs=[pl.BlockSpec((seq, HEAD_DIM), lambda b, h: (b + off, h)),
                  pl.BlockSpec((seq, HEAD_DIM), lambda b, h: (b + off, heads + h)),
                  pl.BlockSpec((seq, HEAD_DIM), lambda b, h: (b + off, 2 * heads + h)),
                  pl.BlockSpec((None, kh, GRID_W, kh * GRID_W), lambda b, h: (h, 0, 0, 0))],
        out_specs=pl.BlockSpec((seq, HEAD_DIM), lambda b, h: (b, h)),
        compiler_params=_params(("parallel", "parallel"), vmem),
        name="neighbourhood_attention",
    )(qkv, qkv, qkv, bias_tab)


DFT_N2 = 64
DFT_CB = 8
LANES = 128


def _cos_sin(num, den):
    ang = (num % den).astype(F32) * (2.0 * math.pi / den)
    return jnp.cos(ang), jnp.sin(ang)


def _dft_stage1_kernel(w_ref, x_ref, o_ref):
    o_ref[...] = jnp.dot(w_ref[...], x_ref[...], preferred_element_type=F32).astype(o_ref.dtype)


def _dft_stage2_kernel(tr_ref, ti_ref, tc_ref, ts_ref, w2_ref, wg_ref, o_ref, *, groups, gdim):
    cb, n2, width = tr_ref.shape
    rep = width // LANES
    xr, xi = [], []
    for c in range(cb):
        tr = tr_ref[c].astype(F32)
        ti = ti_ref[c].astype(F32)
        tc = jnp.tile(tc_ref[c], (1, rep))
        ts = jnp.tile(ts_ref[c], (1, rep))
        stacked = jnp.concatenate([tr * tc + ti * ts, ti * tc - tr * ts], axis=0).astype(BF16)
        x = jnp.dot(w2_ref[...], stacked, preferred_element_type=F32)
        xr.append(x[:n2])
        xi.append(x[n2:])
    xr = jnp.concatenate(xr, axis=0).astype(BF16)
    xi = jnp.concatenate(xi, axis=0).astype(BF16)
    for g in range(groups):
        cols = slice(g * gdim, (g + 1) * gdim)
        f = (jnp.dot(xr[:, cols], wg_ref[0], preferred_element_type=F32)
             + jnp.dot(xi[:, cols], wg_ref[1], preferred_element_type=F32))
        for c in range(cb):
            o_ref[:, c * width + g * gdim:c * width + (g + 1) * gdim] = f[c * n2:(c + 1) * n2].astype(o_ref.dtype)


def fourier_mix(u, *, row_start, batch, seq, group_dim):
    total, width = u.shape
    groups = width // group_dim
    n2 = DFT_N2
    n1 = seq // n2
    cb = min(DFT_CB, n1)
    assert seq % n2 == 0 and n1 % cb == 0 and row_start % seq == 0 and total % n2 == 0
    row = n2 * width
    bn = min(row, 8192)
    a = jnp.arange(n1, dtype=jnp.int32)
    b = jnp.arange(n2, dtype=jnp.int32)
    c1, s1 = _cos_sin(a[:, None] * a[None, :], n1)
    w1 = (jnp.concatenate([c1, -s1], axis=0) * n1 ** -0.5).astype(BF16)
    off = row_start // seq
    t = pl.pallas_call(
        _dft_stage1_kernel,
        out_shape=jax.ShapeDtypeStruct((batch, 2 * n1, row), BF16),
        grid=(batch, row // bn),
        in_specs=[pl.BlockSpec((2 * n1, n1), lambda bi, j: (0, 0)),
                  pl.BlockSpec((n1, bn), lambda bi, j: (off + bi, j))],
        out_specs=pl.BlockSpec((None, 2 * n1, bn), lambda bi, j: (bi, 0, j)),
        compiler_params=_params(("parallel", "parallel"), 2 * (3 * n1 * bn * 2 + 2 * n1 * n1 * 2) + 2 * n1 * bn * 4),
        name="dft_stage1",
    )(w1, u.reshape(total // n2, row))
    t = t.reshape(batch, 2 * n1, n2, width)
    tc, ts = _cos_sin(a[:, None] * b[None, :], seq)
    tc = jnp.broadcast_to(tc[:, :, None], (n1, n2, LANES))
    ts = jnp.broadcast_to(ts[:, :, None], (n1, n2, LANES))
    c2, s2 = _cos_sin(b[:, None] * b[None, :], n2)
    w2 = (jnp.block([[c2, s2], [-s2, c2]]) * n2 ** -0.5).astype(BF16)
    gi = jnp.arange(group_dim, dtype=jnp.int32)
    cg, sg = _cos_sin(gi[:, None] * gi[None, :], group_dim)
    wg = (jnp.stack([cg, sg]) * group_dim ** -0.5).astype(BF16)
    nj = n1 // cb
    vmem = (2 * (2 * cb * n2 * width * 2 + 2 * cb * n2 * LANES * 4 + n2 * cb * width * 2) + 4 * n2 * n2 * 2
            + 2 * wg.size * 2 + 6 * cb * n2 * width * 4)
    out = pl.pallas_call(
        functools.partial(_dft_stage2_kernel, groups=groups, gdim=group_dim),
        out_shape=jax.ShapeDtypeStruct((batch, n2, n1 * width), BF16),
        grid=(batch, nj),
        in_specs=[pl.BlockSpec((None, cb, n2, width), lambda bi, j: (bi, j, 0, 0)),
                  pl.BlockSpec((None, cb, n2, width), lambda bi, j: (bi, nj + j, 0, 0)),
                  pl.BlockSpec((cb, n2, LANES), lambda bi, j: (j, 0, 0)),
                  pl.BlockSpec((cb, n2, LANES), lambda bi, j: (j, 0, 0)),
                  pl.BlockSpec((2 * n2, 2 * n2), lambda bi, j: (0, 0)),
                  pl.BlockSpec((2, group_dim, group_dim), lambda bi, j: (0, 0, 0))],
        out_specs=pl.BlockSpec((None, n2, cb * width), lambda bi, j: (bi, 0, j)),
        compiler_params=_params(("parallel", "parallel"), vmem),
        name="dft_stage2",
    )(t, t, tc, ts, w2, wg)
    return out.reshape(batch * seq, width)


def _t5_bucket(rel):
    nb = T5_BUCKETS // 2
    max_exact = nb // 2
    ret = (rel > 0).astype(jnp.int32) * nb
    n = jnp.abs(rel)
    nf = jnp.maximum(n, 1).astype(F32)
    large = max_exact + (jnp.log(nf / max_exact) / math.log(T5_MAX_DIST / max_exact)
                         * (nb - max_exact)).astype(jnp.int32)
    large = jnp.minimum(large, nb - 1)
    return ret + jnp.where(n < max_exact, n, large)


ATT_BIAS_CHUNKS = (-2, -1, 0, 1, 2)


def _t5_bias_tiles(t5_bias):
    assert ATT_C == ATT_R and ATT_R + 1 >= T5_MAX_DIST
    tiles = []
    for dc in ATT_BIAS_CHUNKS:
        rel = dc * ATT_C + ATT_C - 1 - jnp.arange(ATT_C + ATT_R - 1, dtype=jnp.int32)
        w = t5_bias.astype(F32)[_t5_bucket(rel)].T * LOG2E
        tiles.append(_toeplitz(w, ATT_C, ATT_R))
    return jnp.stack(tiles, axis=1)


ATT_SUM_ROWS = 16

def _diff_attn_kernel(qt_ref, k_ref, vt_ref, nb_ref, lq1_ref, lk1_ref, lq2_ref, lk2_ref, w_ref, o_ref,
                      m_ref, a_ref, s_ref, p_ref, al_ref, *, nc, nsub, lam_init):
    i = pl.program_id(2)
    d = HEAD_DIM
    m_ref[...] = jnp.full(m_ref.shape, NEG_BIG, F32)
    a_ref[...] = jnp.zeros(a_ref.shape, F32)
    lo, hi = ATT_BIAS_CHUNKS[0], ATT_BIAS_CHUNKS[-1]
    units = [(r, half) for r in range(nsub) for half in (0, 1)]

    def scores(c, slot):
        for r, half in units:
            kc = k_ref[pl.ds(pl.multiple_of(c * ATT_C, ATT_C), ATT_C), half * d:(half + 1) * d]
            bias = nb_ref[jnp.clip(c - (i * nsub + r), lo, hi) - lo]
            qt = qt_ref[half * d:(half + 1) * d, r * ATT_R:(r + 1) * ATT_R]
            s_ref[slot, r, half] = jnp.dot(kc, qt, preferred_element_type=F32) + bias

    def softmax(slot):
        for r, half in units:
            s = s_ref[slot, r, half]
            m_old = m_ref[r, half]
            m_new = jnp.maximum(m_old, jnp.max(s, axis=0, keepdims=True))
            m_ref[r, half] = m_new
            al_ref[slot, r, half] = jnp.exp2(m_old - m_new)
            p_ref[slot, r, half] = jnp.exp2(s - m_new).astype(p_ref.dtype)

    def accumulate(c, slot):
        for r, half in units:
            a_ref[r, half] = (al_ref[slot, r, half] * a_ref[r, half]
                              + jnp.dot(vt_ref[c], p_ref[slot, r, half], preferred_element_type=F32))

    def advance(c, t):
        accumulate(c - 2, t)
        softmax(1 - t)
        scores(c, t)

    scores(0, 0)
    scores(1, 1)
    softmax(0)
    for c in range(2, ATT_UNROLL):
        advance(c, c % 2)

    def step(j, carry):
        for t in range(ATT_UNROLL):
            advance(ATT_UNROLL * j + t, t % 2)
        return carry

    lax.fori_loop(1, nc // ATT_UNROLL, step, 0)
    accumulate(nc - 2, 0)
    softmax(1)
    accumulate(nc - 1, 1)

    lam = (jnp.exp(jnp.sum(lq1_ref[...] * lk1_ref[...], axis=-1, keepdims=True))
           - jnp.exp(jnp.sum(lq2_ref[...] * lk2_ref[...], axis=-1, keepdims=True)) + lam_init)
    hd2 = 2 * d
    for r in range(nsub):
        o1, o2 = (a_ref[r, half, :hd2] / a_ref[r, half, hd2:hd2 + 1] for half in (0, 1))
        att = o1 - lam * o2
        inv = lax.rsqrt(jnp.mean(att * att, axis=0, keepdims=True) + SUBLN_EPS)
        y = (att * inv).T * w_ref[...] * (1.0 - lam_init)
        o_ref[r * ATT_R:(r + 1) * ATT_R, :] = y.astype(o_ref.dtype)


def differential_attention(qt, proj, vt, bias_tiles, lam_vecs, subln_w, *, row_start, batch, seq, heads, lam_init):
    tq = min(ATT_TQ, seq)
    nq = seq // tq
    nc = seq // ATT_C
    nsub = tq // ATT_R
    assert seq % tq == 0 and row_start % seq == 0 and tq % ATT_R == 0 and nc % ATT_UNROLL == 0
    qoff = row_start // tq
    soff = row_start // seq
    hd2 = 2 * HEAD_DIM
    rows_a = hd2 + ATT_SUM_ROWS
    vec = pl.BlockSpec((1, HEAD_DIM), lambda b, h, i: (0, 0))
    vmem = (2 * seq * (hd2 + rows_a) * 2 + 2 * bias_tiles[0].size * 4 + 4 * tq * hd2 * 2
            + 2 * nsub * ATT_R * ((rows_a + 3 * 8) * 4 + 2 * ATT_C * 6))
    return pl.pallas_call(
        functools.partial(_diff_attn_kernel, nc=nc, nsub=nsub, lam_init=lam_init),
        out_shape=jax.ShapeDtypeStruct((batch * seq, heads * hd2), BF16),
        grid=(batch, heads, nq),
        in_specs=[pl.BlockSpec((None, hd2, tq), lambda b, h, i: (h, 0, qoff + b * nq + i)),
                  pl.BlockSpec((seq, hd2), lambda b, h, i: (soff + b, heads + h)),
                  pl.BlockSpec((None, nc, rows_a, ATT_C), lambda b, h, i: (h, soff + b, 0, 0)),
                  pl.BlockSpec((None,) + bias_tiles.shape[1:], lambda b, h, i: (h, 0, 0, 0)),
                  vec, vec, vec, vec,
                  pl.BlockSpec((1, hd2), lambda b, h, i: (0, 0))],
        out_specs=pl.BlockSpec((tq, hd2), lambda b, h, i: (b * nq + i, h)),
        scratch_shapes=[pltpu.VMEM((nsub, 2, 1, ATT_R), F32), pltpu.VMEM((nsub, 2, rows_a, ATT_R), F32),
                        pltpu.VMEM((2, nsub, 2, ATT_C, ATT_R), F32), pltpu.VMEM((2, nsub, 2, ATT_C, ATT_R), BF16),
                        pltpu.VMEM((2, nsub, 2, 1, ATT_R), F32)],
        compiler_params=_params(("parallel", "parallel", "arbitrary"), vmem),
        name="differential_attention",
    )(qt, proj, vt, bias_tiles, *lam_vecs, subln_w.reshape(1, hd2).astype(F32))


def _cast_pad_kernel(x_ref, o_ref, *, row_blocks, pad_rows):
    cols = x_ref.shape[1]

    def copy():
        o_ref[:, :cols] = x_ref[...].astype(o_ref.dtype)
        if cols < o_ref.shape[1]:
            o_ref[:, cols:] = jnp.zeros((o_ref.shape[0], o_ref.shape[1] - cols), o_ref.dtype)

    if not pad_rows:
        copy()
    else:
        pl.when(pl.program_id(0) < row_blocks)(copy)

        @pl.when(pl.program_id(0) >= row_blocks)
        def _():
            o_ref[...] = jnp.zeros(o_ref.shape, o_ref.dtype)


def cast_pad(w, layer, rows_to, cols_to, *, bm=LANES):
    _, rows, cols = w.shape
    assert rows % bm == 0 and rows_to % bm == 0 and cols % LANES == 0 and cols_to % LANES == 0
    row_blocks = rows // bm
    return pl.pallas_call(
        functools.partial(_cast_pad_kernel, row_blocks=row_blocks, pad_rows=rows_to > rows),
        out_shape=jax.ShapeDtypeStruct((rows_to, cols_to), BF16),
        grid=(rows_to // bm,),
        in_specs=[pl.BlockSpec((None, bm, cols), lambda i: (layer, jnp.minimum(i, row_blocks - 1), 0))],
        out_specs=pl.BlockSpec((bm, cols_to), lambda i: (i, 0)),
        compiler_params=_params(("parallel",), 2 * bm * (cols * 4 + cols_to * 2)),
        name="cast_pad",
    )(w)


def _round_up(n, mult):
    return -(-n // mult) * mult


def kernel(x_prompt, x_sample, norm_mix, norm_ffn, w_in_even, rpb_na, w_out_even, w_in_odd,
           lambda_q1, lambda_k1, lambda_q2, lambda_k2, subln_w, w_out_odd, t5_bias,
           w_gate, w_up, w_down, norm_final):
    d_model = x_prompt.shape[-1]
    depth = norm_mix.shape[0]
    na_heads = rpb_na.shape[1]
    na_width = na_heads * HEAD_DIM
    diff_heads = t5_bias.shape[1]
    diff_width = diff_heads * 2 * HEAD_DIM
    fnet_width = w_in_even.shape[-1] - 3 * na_width
    group_dim = fnet_width // FNET_GROUPS
    hidden = _round_up(w_gate.shape[-1], FFN_PAD)

    layers = []
    for layer in range(depth):
        lw = {}
        if layer % 2 == 0:
            e = layer // 2
            w_in = w_in_even[e].astype(BF16)
            lw.update(w_qkv=w_in[:, :3 * na_width], w_u=w_in[:, 3 * na_width:], w_out=w_out_even[e].astype(BF16),
                      rpb=rpb_na[e])
        else:
            o = layer // 2
            col_scale = jnp.where(jnp.arange(3 * diff_width) < diff_width, HEAD_DIM ** -0.5 * LOG2E, 1.0)
            lw.update(w_in=(w_in_odd[o] * col_scale.astype(F32)).astype(BF16), w_out=w_out_odd[o].astype(BF16),
                      bias_tiles=_t5_bias_tiles(t5_bias), subln_w=subln_w[o],
                      lam_vecs=[v[o].reshape(1, HEAD_DIM).astype(F32)
                                for v in (lambda_q1, lambda_k1, lambda_q2, lambda_k2)])
        lw.update(wg=cast_pad(w_gate, layer, d_model, hidden), wu=cast_pad(w_up, layer, d_model, hidden),
                  wd=cast_pad(w_down, layer, hidden, d_model))
        layers.append(lw)

    return tuple(_trunk(xb, layers, norm_mix, norm_ffn, norm_final, na_heads=na_heads, diff_heads=diff_heads,
                        group_dim=group_dim) for xb in (x_prompt, x_sample))


def _trunk(xb, layers, norm_mix, norm_ffn, norm_final, *, na_heads, diff_heads, group_dim):
    batch, seq, d_model = xb.shape
    diff_width = diff_heads * 2 * HEAD_DIM
    x = xb.reshape(batch * seq, d_model)
    for layer, lw in enumerate(layers):
        h = rmsnorm(x, norm_mix[layer], BF16)
        if layer % 2 == 0:
            qkv = matmul(h, lw["w_qkv"], BF16)
            u = matmul(h, lw["w_u"], BF16)
            kh = min(NA_ROWS, seq // GRID_W)
            oa = neighbourhood_attention(qkv, _na_bias_table(lw["rpb"], kh), row_start=0, batch=batch, seq=seq,
                                         heads=na_heads)
            ob = fourier_mix(u, row_start=0, batch=batch, seq=seq, group_dim=group_dim)
            x = matmul([oa, ob], lw["w_out"], F32, residual=x)
        else:
            proj = matmul(h, lw["w_in"], BF16)
            qt = jnp.transpose(proj[:, :diff_width].reshape(-1, diff_heads, 2 * HEAD_DIM), (1, 2, 0))
            vt = proj[:, 2 * diff_width:].reshape(-1, ATT_C, diff_heads, 2 * HEAD_DIM)
            vt = jnp.transpose(vt, (2, 0, 3, 1))
            vt = jnp.concatenate([vt, jnp.ones(vt.shape[:2] + (ATT_SUM_ROWS, ATT_C), BF16)], axis=2)
            lam_init = 0.8 - 0.6 * math.exp(-0.3 * layer)
            att = differential_attention(qt, proj, vt, lw["bias_tiles"], lw["lam_vecs"], lw["subln_w"], row_start=0,
                                         batch=batch, seq=seq, heads=diff_heads, lam_init=lam_init)
            x = matmul(att, lw["w_out"], F32, residual=x)
        h = rmsnorm(x, norm_ffn[layer], BF16)
        act = gate_up(h, lw["wg"], lw["wu"])
        x = matmul(act, lw["wd"], F32, residual=x, bn=2048, tk=1024)
    return rmsnorm(x, norm_final, F32).reshape(xb.shape)
```

```python
import functools
import math

import jax
import jax.numpy as jnp
from jax import lax
from jax.experimental import pallas as pl
from jax.experimental.pallas import tpu as pltpu

F32 = jnp.float32
BF16 = jnp.bfloat16

HEAD_DIM = 128
GRID_W = 64
NA_ROWS = 8
NA_COLS = 16
FNET_GROUPS = 4
T5_BUCKETS = 32
T5_MAX_DIST = 128
RMS_EPS = 1e-6
SUBLN_EPS = 1e-5
NEG_BIG = -1e30

V7X_VMEM_LIMIT = 56 * 1024 * 1024
ATT_R = 256
ATT_C = 256
ATT_TQ = 1024
ATT_UNROLL = 2
NA_UNROLL = 16
FFN_PAD = 1024
LOG2E = math.log2(math.e)


def _toeplitz(w, rows, cols):
    lead = w.shape[:-1]
    period = rows + cols
    wp = jnp.concatenate([w, jnp.zeros(lead + (1,), w.dtype)], axis=-1)
    flat = jnp.tile(wp, (1,) * len(lead) + (rows,))[..., :rows * (period - 1)]
    return flat.reshape(lead + (rows, period - 1))[..., rows - 1:rows - 1 + cols]


def _params(sem, vmem_bytes):
    return pltpu.CompilerParams(dimension_semantics=sem,
                                vmem_limit_bytes=int(min(V7X_VMEM_LIMIT, vmem_bytes * 3 // 2)))


def _rmsnorm_kernel(x_ref, g_ref, o_ref, *, eps):
    x = x_ref[...]
    inv = lax.rsqrt(jnp.mean(x * x, axis=-1, keepdims=True) + eps)
    o_ref[...] = (x * inv * g_ref[...]).astype(o_ref.dtype)


def rmsnorm(x, g, out_dtype, *, row_start=0, rows=None, bm=256):
    total, d = x.shape
    rows = total if rows is None else rows
    off = row_start // bm
    assert row_start % bm == 0 and rows % bm == 0
    return pl.pallas_call(
        functools.partial(_rmsnorm_kernel, eps=RMS_EPS),
        out_shape=jax.ShapeDtypeStruct((rows, d), out_dtype),
        grid=(rows // bm,),
        in_specs=[pl.BlockSpec((bm, d), lambda i: (i + off, 0)),
                  pl.BlockSpec((1, d), lambda i: (0, 0))],
        out_specs=pl.BlockSpec((bm, d), lambda i: (i, 0)),
        compiler_params=_params(("parallel",), 6 * bm * d * 4),
        name="rmsnorm",
    )(x, g.reshape(1, d).astype(F32))


def _matmul_kernel(*refs, nk, na, has_res):
    a_refs, w_ref = refs[:na], refs[na]
    r_ref = refs[na + 1] if has_res else None
    o_ref = refs[-1]

    def part():
        acc, start = None, 0
        for a_ref in a_refs:
            stop = start + a_ref.shape[1]
            d = jnp.dot(a_ref[...], w_ref[start:stop, :], preferred_element_type=F32)
            acc = d if acc is None else acc + d
            start = stop
        return acc

    if nk == 1:
        o_ref[...] = (part() + r_ref[...] if r_ref is not None else part()).astype(o_ref.dtype)
        return
    k = pl.program_id(2)

    @pl.when(k == 0)
    def _():
        o_ref[...] = part() + r_ref[...] if r_ref is not None else part()

    @pl.when(k > 0)
    def _():
        o_ref[...] += part()


def matmul(a, w, out_dtype, *, residual=None, bm=1024, bn=1024, tk=None):
    a_list = list(a) if isinstance(a, (list, tuple)) else [a]
    m = a_list[0].shape[0]
    kdim, n = w.shape
    assert sum(x.shape[1] for x in a_list) == kdim
    tk = kdim if tk is None else tk
    bm, bn = min(bm, m), min(bn, n)
    assert m % bm == 0 and n % bn == 0 and kdim % tk == 0
    nk = kdim // tk
    assert nk == 1 or len(a_list) == 1
    has_res = residual is not None
    in_specs = [pl.BlockSpec((bm, x.shape[1] // nk), lambda i, j, k: (i, k)) for x in a_list]
    in_specs.append(pl.BlockSpec((tk, bn), lambda i, j, k: (k, j)))
    args = a_list + [w]
    if has_res:
        in_specs.append(pl.BlockSpec((bm, bn), lambda i, j, k: (i, j)))
        args.append(residual)
    assert nk == 1 or out_dtype == F32
    out_bytes = jnp.dtype(out_dtype).itemsize
    vmem = (2 * (bm * tk + tk * bn) * 2 + 2 * bm * bn * out_bytes + (2 * bm * bn * 4 if has_res else 0)
            + bm * bn * 4)
    return pl.pallas_call(
        functools.partial(_matmul_kernel, nk=nk, na=len(a_list), has_res=has_res),
        out_shape=jax.ShapeDtypeStruct((m, n), out_dtype),
        grid=(m // bm, n // bn, nk),
        in_specs=in_specs,
        out_specs=pl.BlockSpec((bm, bn), lambda i, j, k: (i, j)),
        compiler_params=_params(("parallel", "parallel", "arbitrary"), vmem),
        name="matmul",
    )(*args)


def _gate_up_kernel(a_ref, wg_ref, wu_ref, o_ref):
    a = a_ref[...]
    g = jnp.dot(a, wg_ref[...], preferred_element_type=F32)
    u = jnp.dot(a, wu_ref[...], preferred_element_type=F32)
    o_ref[...] = (g * jax.nn.sigmoid(g) * u).astype(o_ref.dtype)


def gate_up(h, wg, wu, *, bm=1024, bn=512):
    m, kdim = h.shape
    _, n = wg.shape
    assert m % bm == 0 and n % bn == 0
    vmem = 2 * bm * kdim * 2 + 4 * kdim * bn * 2 + 2 * bm * bn * 2 + 4 * bm * bn * 4
    return pl.pallas_call(
        _gate_up_kernel,
        out_shape=jax.ShapeDtypeStruct((m, n), BF16),
        grid=(m // bm, n // bn),
        in_specs=[pl.BlockSpec((bm, kdim), lambda i, j: (i, 0)),
                  pl.BlockSpec((kdim, bn), lambda i, j: (0, j)),
                  pl.BlockSpec((kdim, bn), lambda i, j: (0, j))],
        out_specs=pl.BlockSpec((bm, bn), lambda i, j: (i, j)),
        compiler_params=_params(("parallel", "parallel"), vmem),
        name="gate_up",
    )(h, wg, wu)


def _na_kernel(q_ref, k_ref, v_ref, bias_ref, o_ref, *, rows, kh):
    scale = HEAD_DIM ** -0.5
    win = kh * GRID_W

    def row_group(g, carry):
        rws = [g * NA_UNROLL + t for t in range(NA_UNROLL)]
        starts = [jnp.clip(r - kh // 2, 0, rows - kh) for r in rws]
        qoffs = [pl.multiple_of(r * GRID_W, GRID_W) for r in rws]
        koffs = [pl.multiple_of(rs * GRID_W, GRID_W) for rs in starts]
        ss = [lax.dot_general(q_ref[pl.ds(qo, GRID_W), :], k_ref[pl.ds(ko, win), :], (((1,), (1,)), ((), ())),
                              preferred_element_type=F32) for qo, ko in zip(qoffs, koffs)]
        ps, ls = [], []
        for s, r, rs in zip(ss, rws, starts):
            s = s * scale + bias_ref[r - rs]
            p = jnp.exp(s - jnp.max(s, axis=-1, keepdims=True))
            ls.append(jnp.sum(p, axis=-1, keepdims=True))
            ps.append(p.astype(v_ref.dtype))
        os_ = [jnp.dot(p, v_ref[pl.ds(ko, win), :], preferred_element_type=F32) for p, ko in zip(ps, koffs)]
        for o, l, qo in zip(os_, ls, qoffs):
            o_ref[pl.ds(qo, GRID_W), :] = (o / l).astype(o_ref.dtype)
        return carry

    assert rows % NA_UNROLL == 0
    lax.fori_loop(0, rows // NA_UNROLL, row_group, 0)


def _na_bias_table(rpb, kh):
    heads = rpb.shape[0]
    c = jnp.arange(GRID_W, dtype=jnp.int32)
    cs = jnp.clip(c - NA_COLS // 2, 0, GRID_W - NA_COLS)
    j = jnp.arange(GRID_W, dtype=jnp.int32)
    valid = (j[None, :] >= cs[:, None]) & (j[None, :] < cs[:, None] + NA_COLS)
    pad = GRID_W - NA_COLS
    w = jnp.pad(rpb.astype(F32), ((0, 0), (0, 0), (pad, pad)))
    t = jnp.where(valid, _toeplitz(w, GRID_W, GRID_W), NEG_BIG)
    t = jnp.transpose(t, (0, 2, 1, 3))
    tabs = [t[:, :, NA_ROWS - 1 - dl:NA_ROWS - 1 - dl + kh, :].reshape(heads, GRID_W, kh * GRID_W)
            for dl in range(kh)]
    return jnp.stack(tabs, axis=1)


def neighbourhood_attention(qkv, bias_tab, *, row_start, batch, seq, heads):
    rows = seq // GRID_W
    kh = bias_tab.shape[1]
    off = row_start // seq
    assert row_start % seq == 0
    vmem = 2 * 4 * seq * HEAD_DIM * 2 + 2 * bias_tab[0].size * 4 + (8 << 20)
    return pl.pallas_call(
        functools.partial(_na_kernel, rows=rows, kh=kh),
        out_shape=jax.ShapeDtypeStruct((batch * seq, heads * HEAD_DIM), BF16),
        grid=(batch, heads),
        in_specs=[pl.BlockSpec((seq, HEAD_DIM), lambda b, h: (b + off, h)),
                  pl.BlockSpec((seq, HEAD_DIM), lambda b, h: (b + off, heads + h)),
                  pl.BlockSpec((seq, HEAD_DIM), lambda b, h: (b + off, 2 * heads + h)),
                  pl.BlockSpec((None, kh, GRID_W, kh * GRID_W), lambda b, h: (h, 0, 0, 0))],
        out_specs=pl.BlockSpec((seq, HEAD_DIM), lambda b, h: (b, h)),
        compiler_params=_params(("parallel", "parallel"), vmem),
        name="neighbourhood_attention",
    )(qkv, qkv, qkv, bias_tab)


DFT_N2 = 64
DFT_CB = 8
LANES = 128


def _cos_sin(num, den):
    ang = (num % den).astype(F32) * (2.0 * math.pi / den)
    return jnp.cos(ang), jnp.sin(ang)


def _dft_stage1_kernel(w_ref, x_ref, o_ref):
    o_ref[...] = jnp.dot(w_ref[...], x_ref[...], preferred_element_type=F32).astype(o_ref.dtype)


def _dft_stage2_kernel(tr_ref, ti_ref, tc_ref, ts_ref, w2_ref, wg_ref, o_ref, *, groups, gdim):
    cb, n2, width = tr_ref.shape
    rep = width // LANES
    xr, xi = [], []
    for c in range(cb):
        tr = tr_ref[c].astype(F32)
        ti = ti_ref[c].astype(F32)
        tc = jnp.tile(tc_ref[c], (1, rep))
        ts = jnp.tile(ts_ref[c], (1, rep))
        stacked = jnp.concatenate([tr * tc + ti * ts, ti * tc - tr * ts], axis=0).astype(BF16)
        x = jnp.dot(w2_ref[...], stacked, preferred_element_type=F32)
        xr.append(x[:n2])
        xi.append(x[n2:])
    xr = jnp.concatenate(xr, axis=0).astype(BF16)
    xi = jnp.concatenate(xi, axis=0).astype(BF16)
    for g in range(groups):
        cols = slice(g * gdim, (g + 1) * gdim)
        f = (jnp.dot(xr[:, cols], wg_ref[0], preferred_element_type=F32)
             + jnp.dot(xi[:, cols], wg_ref[1], preferred_element_type=F32))
        for c in range(cb):
            o_ref[:, c * width + g * gdim:c * width + (g + 1) * gdim] = f[c * n2:(c + 1) * n2].astype(o_ref.dtype)


def fourier_mix(u, *, row_start, batch, seq, group_dim):
    total, width = u.shape
    groups = width // group_dim
    n2 = DFT_N2
    n1 = seq // n2
    cb = min(DFT_CB, n1)
    assert seq % n2 == 0 and n1 % cb == 0 and row_start % seq == 0 and total % n2 == 0
    row = n2 * width
    bn = min(row, 8192)
    a = jnp.arange(n1, dtype=jnp.int32)
    b = jnp.arange(n2, dtype=jnp.int32)
    c1, s1 = _cos_sin(a[:, None] * a[None, :], n1)
    w1 = (jnp.concatenate([c1, -s1], axis=0) * n1 ** -0.5).astype(BF16)
    off = row_start // seq
    t = pl.pallas_call(
        _dft_stage1_kernel,
        out_shape=jax.ShapeDtypeStruct((batch, 2 * n1, row), BF16),
        grid=(batch, row // bn),
        in_specs=[pl.BlockSpec((2 * n1, n1), lambda bi, j: (0, 0)),
                  pl.BlockSpec((n1, bn), lambda bi, j: (off + bi, j))],
        out_specs=pl.BlockSpec((None, 2 * n1, bn), lambda bi, j: (bi, 0, j)),
        compiler_params=_params(("parallel", "parallel"), 2 * (3 * n1 * bn * 2 + 2 * n1 * n1 * 2) + 2 * n1 * bn * 4),
        name="dft_stage1",
    )(w1, u.reshape(total // n2, row))
    t = t.reshape(batch, 2 * n1, n2, width)
    tc, ts = _cos_sin(a[:, None] * b[None, :], seq)
    tc = jnp.broadcast_to(tc[:, :, None], (n1, n2, LANES))
    ts = jnp.broadcast_to(ts[:, :, None], (n1, n2, LANES))
    c2, s2 = _cos_sin(b[:, None] * b[None, :], n2)
    w2 = (jnp.block([[c2, s2], [-s2, c2]]) * n2 ** -0.5).astype(BF16)
    gi = jnp.arange(group_dim, dtype=jnp.int32)
    cg, sg = _cos_sin(gi[:, None] * gi[None, :], group_dim)
    wg = (jnp.stack([cg, sg]) * group_dim ** -0.5).astype(BF16)
    nj = n1 // cb
    vmem = (2 * (2 * cb * n2 * width * 2 + 2 * cb * n2 * LANES * 4 + n2 * cb * width * 2) + 4 * n2 * n2 * 2
            + 2 * wg.size * 2 + 6 * cb * n2 * width * 4)
    out = pl.pallas_call(
        functools.partial(_dft_stage2_kernel, groups=groups, gdim=group_dim),
        out_shape=jax.ShapeDtypeStruct((batch, n2, n1 * width), BF16),
        grid=(batch, nj),
        in_specs=[pl.BlockSpec((None, cb, n2, width), lambda bi, j: (bi, j, 0, 0)),
                  pl.BlockSpec((None, cb, n2, width), lambda bi, j: (bi, nj + j, 0, 0)),
                  pl.BlockSpec((cb, n2, LANES), lambda bi, j: (j, 0, 0)),
                  pl.BlockSpec((cb, n2, LANES), lambda bi, j: (j, 0, 0)),
                  pl.BlockSpec((2 * n2, 2 * n2), lambda bi, j: (0, 0)),
                  pl.BlockSpec((2, group_dim, group_dim), lambda bi, j: (0, 0, 0))],
        out_specs=pl.BlockSpec((None, n2, cb * width), lambda bi, j: (bi, 0, j)),
        compiler_params=_params(("parallel", "parallel"), vmem),
        name="dft_stage2",
    )(t, t, tc, ts, w2, wg)
    return out.reshape(batch * seq, width)


def _t5_bucket(rel):
    nb = T5_BUCKETS // 2
    max_exact = nb // 2
    ret = (rel > 0).astype(jnp.int32) * nb
    n = jnp.abs(rel)
    nf = jnp.maximum(n, 1).astype(F32)
    large = max_exact + (jnp.log(nf / max_exact) / math.log(T5_MAX_DIST / max_exact)
                         * (nb - max_exact)).astype(jnp.int32)
    large = jnp.minimum(large, nb - 1)
    return ret + jnp.where(n < max_exact, n, large)


ATT_BIAS_CHUNKS = (-2, -1, 0, 1, 2)


def _t5_bias_tiles(t5_bias):
    assert ATT_C == ATT_R and ATT_R + 1 >= T5_MAX_DIST
    tiles = []
    for dc in ATT_BIAS_CHUNKS:
        rel = dc * ATT_C + ATT_C - 1 - jnp.arange(ATT_C + ATT_R - 1, dtype=jnp.int32)
        w = t5_bias.astype(F32)[_t5_bucket(rel)].T * LOG2E
        tiles.append(_toeplitz(w, ATT_C, ATT_R))
    return jnp.stack(tiles, axis=1)


ATT_SUM_ROWS = 16

def _diff_attn_kernel(qt_ref, k_ref, vt_ref, nb_ref, lq1_ref, lk1_ref, lq2_ref, lk2_ref, w_ref, o_ref,
                      m_ref, a_ref, s_ref, p_ref, al_ref, *, nc, nsub, lam_init):
    i = pl.program_id(2)
    d = HEAD_DIM
    m_ref[...] = jnp.full(m_ref.shape, NEG_BIG, F32)
    a_ref[...] = jnp.zeros(a_ref.shape, F32)
    lo, hi = ATT_BIAS_CHUNKS[0], ATT_BIAS_CHUNKS[-1]
    units = [(r, half) for r in range(nsub) for half in (0, 1)]

    def scores(c, slot):
        for r, half in units:
            kc = k_ref[pl.ds(pl.multiple_of(c * ATT_C, ATT_C), ATT_C), half * d:(half + 1) * d]
            bias = nb_ref[jnp.clip(c - (i * nsub + r), lo, hi) - lo]
            qt = qt_ref[half * d:(half + 1) * d, r * ATT_R:(r + 1) * ATT_R]
            s_ref[slot, r, half] = jnp.dot(kc, qt, preferred_element_type=F32) + bias

    def softmax(slot):
        for r, half in units:
            s = s_ref[slot, r, half]
            m_old = m_ref[r, half]
            m_new = jnp.maximum(m_old, jnp.max(s, axis=0, keepdims=True))
            m_ref[r, half] = m_new
            al_ref[slot, r, half] = jnp.exp2(m_old - m_new)
            p_ref[slot, r, half] = jnp.exp2(s - m_new).astype(p_ref.dtype)

    def accumulate(c, slot):
        for r, half in units:
            a_ref[r, half] = (al_ref[slot, r, half] * a_ref[r, half]
                              + jnp.dot(vt_ref[c], p_ref[slot, r, half], preferred_element_type=F32))

    def advance(c, t):
        accumulate(c - 2, t)
        softmax(1 - t)
        scores(c, t)

    scores(0, 0)
    scores(1, 1)
    softmax(0)
    for c in range(2, ATT_UNROLL):
        advance(c, c % 2)

    def step(j, carry):
        for t in range(ATT_UNROLL):
            advance(ATT_UNROLL * j + t, t % 2)
        return carry

    lax.fori_loop(1, nc // ATT_UNROLL, step, 0)
    accumulate(nc - 2, 0)
    softmax(1)
    accumulate(nc - 1, 1)

    lam = (jnp.exp(jnp.sum(lq1_ref[...] * lk1_ref[...], axis=-1, keepdims=True))
           - jnp.exp(jnp.sum(lq2_ref[...] * lk2_ref[...], axis=-1, keepdims=True)) + lam_init)
    hd2 = 2 * d
    for r in range(nsub):
        o1, o2 = (a_ref[r, half, :hd2] / a_ref[r, half, hd2:hd2 + 1] for half in (0, 1))
        att = o1 - lam * o2
        inv = lax.rsqrt(jnp.mean(att * att, axis=0, keepdims=True) + SUBLN_EPS)
        y = (att * inv).T * w_ref[...] * (1.0 - lam_init)
        o_ref[r * ATT_R:(r + 1) * ATT_R, :] = y.astype(o_ref.dtype)


def differential_attention(qt, proj, vt, bias_tiles, lam_vecs, subln_w, *, row_start, batch, seq, heads, lam_init):
    tq = min(ATT_TQ, seq)
    nq = seq // tq
    nc = seq // ATT_C
    nsub = tq // ATT_R
    assert seq % tq == 0 and row_start % seq == 0 and tq % ATT_R == 0 and nc % ATT_UNROLL == 0
    qoff = row_start // tq
    soff = row_start // seq
    hd2 = 2 * HEAD_DIM
    rows_a = hd2 + ATT_SUM_ROWS
    vec = pl.BlockSpec((1, HEAD_DIM), lambda b, h, i: (0, 0))
    vmem = (2 * seq * (hd2 + rows_a) * 2 + 2 * bias_tiles[0].size * 4 + 4 * tq * hd2 * 2
            + 2 * nsub * ATT_R * ((rows_a + 3 * 8) * 4 + 2 * ATT_C * 6))
    return pl.pallas_call(
        functools.partial(_diff_attn_kernel, nc=nc, nsub=nsub, lam_init=lam_init),
        out_shape=jax.ShapeDtypeStruct((batch * seq, heads * hd2), BF16),
        grid=(batch, heads, nq),
        in_specs=[pl.BlockSpec((None, hd2, tq), lambda b, h, i: (h, 0, qoff + b * nq + i)),
                  pl.BlockSpec((seq, hd2), lambda b, h, i: (soff + b, heads + h)),
                  pl.BlockSpec((None, nc, rows_a, ATT_C), lambda b, h, i: (h, soff + b, 0, 0)),
                  pl.BlockSpec((None,) + bias_tiles.shape[1:], lambda b, h, i: (h, 0, 0, 0)),
                  vec, vec, vec, vec,
                  pl.BlockSpec((1, hd2), lambda b, h, i: (0, 0))],
        out_specs=pl.BlockSpec((tq, hd2), lambda b, h, i: (b * nq + i, h)),
        scratch_shapes=[pltpu.VMEM((nsub, 2, 1, ATT_R), F32), pltpu.VMEM((nsub, 2, rows_a, ATT_R), F32),
                        pltpu.VMEM((2, nsub, 2, ATT_C, ATT_R), F32), pltpu.VMEM((2, nsub, 2, ATT_C, ATT_R), BF16),
                        pltpu.VMEM((2, nsub, 2, 1, ATT_R), F32)],
        compiler_params=_params(("parallel", "parallel", "arbitrary"), vmem),
        name="differential_attention",
    )(qt, proj, vt, bias_tiles, *lam_vecs, subln_w.reshape(1, hd2).astype(F32))


def _cast_pad_kernel(x_ref, o_ref, *, row_blocks, pad_rows):
    cols = x_ref.shape[1]

    def copy():
        o_ref[:, :cols] = x_ref[...].astype(o_ref.dtype)
        if cols < o_ref.shape[1]:
            o_ref[:, cols:] = jnp.zeros((o_ref.shape[0], o_ref.shape[1] - cols), o_ref.dtype)

    if not pad_rows:
        copy()
    else:
        pl.when(pl.program_id(0) < row_blocks)(copy)

        @pl.when(pl.program_id(0) >= row_blocks)
        def _():
            o_ref[...] = jnp.zeros(o_ref.shape, o_ref.dtype)


def cast_pad(w, layer, rows_to, cols_to, *, bm=LANES):
    _, rows, cols = w.shape
    assert rows % bm == 0 and rows_to % bm == 0 and cols % LANES == 0 and cols_to % LANES == 0
    row_blocks = rows // bm
    return pl.pallas_call(
        functools.partial(_cast_pad_kernel, row_blocks=row_blocks, pad_rows=rows_to > rows),
        out_shape=jax.ShapeDtypeStruct((rows_to, cols_to), BF16),
        grid=(rows_to // bm,),
        in_specs=[pl.BlockSpec((None, bm, cols), lambda i: (layer, jnp.minimum(i, row_blocks - 1), 0))],
        out_specs=pl.BlockSpec((bm, cols_to), lambda i: (i, 0)),
        compiler_params=_params(("parallel",), 2 * bm * (cols * 4 + cols_to * 2)),
        name="cast_pad",
    )(w)


def _round_up(n, mult):
    return -(-n // mult) * mult


def kernel(x_prompt, x_sample, norm_mix, norm_ffn, w_in_even, rpb_na, w_out_even, w_in_odd,
           lambda_q1, lambda_k1, lambda_q2, lambda_k2, subln_w, w_out_odd, t5_bias,
           w_gate, w_up, w_down, norm_final):
    d_model = x_prompt.shape[-1]
    depth = norm_mix.shape[0]
    na_heads = rpb_na.shape[1]
    na_width = na_heads * HEAD_DIM
    diff_heads = t5_bias.shape[1]
    diff_width = diff_heads * 2 * HEAD_DIM
    fnet_width = w_in_even.shape[-1] - 3 * na_width
    group_dim = fnet_width // FNET_GROUPS
    hidden = _round_up(w_gate.shape[-1], FFN_PAD)

    layers = []
    for layer in range(depth):
        lw = {}
        if layer % 2 == 0:
            e = layer // 2
            w_in = w_in_even[e].astype(BF16)
            lw.update(w_qkv=w_in[:, :3 * na_width], w_u=w_in[:, 3 * na_width:], w_out=w_out_even[e].astype(BF16),
                      rpb=rpb_na[e])
        else:
            o = layer // 2
            col_scale = jnp.where(jnp.arange(3 * diff_width) < diff_width, HEAD_DIM ** -0.5 * LOG2E, 1.0)
            lw.update(w_in=(w_in_odd[o] * col_scale.astype(F32)).astype(BF16), w_out=w_out_odd[o].astype(BF16),
                      bias_tiles=_t5_bias_tiles(t5_bias), subln_w=subln_w[o],
                      lam_vecs=[v[o].reshape(1, HEAD_DIM).astype(F32)
                                for v in (lambda_q1, lambda_k1, lambda_q2, lambda_k2)])
        lw.update(wg=cast_pad(w_gate, layer, d_model, hidden), wu=cast_pad(w_up, layer, d_model, hidden),
                  wd=cast_pad(w_down, layer, hidden, d_model))
        layers.append(lw)

    return tuple(_trunk(xb, layers, norm_mix, norm_ffn, norm_final, na_heads=na_heads, diff_heads=diff_heads,
                        group_dim=group_dim) for xb in (x_prompt, x_sample))


def _trunk(xb, layers, norm_mix, norm_ffn, norm_final, *, na_heads, diff_heads, group_dim):
    batch, seq, d_model = xb.shape
    diff_width = diff_heads * 2 * HEAD_DIM
    x = xb.reshape(batch * seq, d_model)
    for layer, lw in enumerate(layers):
        h = rmsnorm(x, norm_mix[layer], BF16)
        if layer % 2 == 0:
            qkv = matmul(h, lw["w_qkv"], BF16)
            u = matmul(h, lw["w_u"], BF16)
            kh = min(NA_ROWS, seq // GRID_W)
            oa = neighbourhood_attention(qkv, _na_bias_table(lw["rpb"], kh), row_start=0, batch=batch, seq=seq,
                                         heads=na_heads)
            ob = fourier_mix(u, row_start=0, batch=batch, seq=seq, group_dim=group_dim)
            x = matmul([oa, ob], lw["w_out"], F32, residual=x)
        else:
            proj = matmul(h, lw["w_in"], BF16)
            qt = jnp.transpose(proj[:, :diff_width].reshape(-1, diff_heads, 2 * HEAD_DIM), (1, 2, 0))
            vt = proj[:, 2 * diff_width:].reshape(-1, ATT_C, diff_heads, 2 * HEAD_DIM)
            vt = jnp.transpose(vt, (2, 0, 3, 1))
            vt = jnp.concatenate([vt, jnp.ones(vt.shape[:2] + (ATT_SUM_ROWS, ATT_C), BF16)], axis=2)
            lam_init = 0.8 - 0.6 * math.exp(-0.3 * layer)
            att = differential_attention(qt, proj, vt, lw["bias_tiles"], lw["lam_vecs"], lw["subln_w"], row_start=0,
                                         batch=batch, seq=seq, heads=diff_heads, lam_init=lam_init)
            x = matmul(att, lw["w_out"], F32, residual=x)
        h = rmsnorm(x, norm_ffn[layer], BF16)
        act = gate_up(h, lw["wg"], lw["wu"])
        x = matmul(act, lw["wd"], F32, residual=x, bn=2048, tk=1024)
    return rmsnorm(x, norm_final, F32).reshape(xb.shape)
```

```python
import functools
import math

import jax
import jax.numpy as jnp
from jax import lax
from jax.experimental import pallas as pl
from jax.experimental.pallas import tpu as pltpu

F32 = jnp.float32
BF16 = jnp.bfloat16

HEAD_DIM = 128
GRID_W = 64
NA_ROWS = 8
NA_COLS = 16
FNET_GROUPS = 4
T5_BUCKETS = 32
T5_MAX_DIST = 128
RMS_EPS = 1e-6
SUBLN_EPS = 1e-5
NEG_BIG = -1e30

V7X_VMEM_LIMIT = 56 * 1024 * 1024
ATT_R = 256
ATT_C = 256
ATT_TQ = 512
ATT_UNROLL = 4
NA_UNROLL = 8
FFN_PAD = 1024
LOG2E = math.log2(math.e)


def _toeplitz(w, rows, cols):
    lead = w.shape[:-1]
    period = rows + cols
    wp = jnp.concatenate([w, jnp.zeros(lead + (1,), w.dtype)], axis=-1)
    flat = jnp.tile(wp, (1,) * len(lead) + (rows,))[..., :rows * (period - 1)]
    return flat.reshape(lead + (rows, period - 1))[..., rows - 1:rows - 1 + cols]


def _params(sem, vmem_bytes):
    return pltpu.CompilerParams(dimension_semantics=sem,
                                vmem_limit_bytes=int(min(V7X_VMEM_LIMIT, vmem_bytes * 3 // 2)))


def _rmsnorm_kernel(x_ref, g_ref, o_ref, *, eps):
    x = x_ref[...]
    inv = lax.rsqrt(jnp.mean(x * x, axis=-1, keepdims=True) + eps)
    o_ref[...] = (x * inv * g_ref[...]).astype(o_ref.dtype)


def rmsnorm(x, g, out_dtype, *, row_start=0, rows=None, bm=256):
    total, d = x.shape
    rows = total if rows is None else rows
    off = row_start // bm
    assert row_start % bm == 0 and rows % bm == 0
    return pl.pallas_call(
        functools.partial(_rmsnorm_kernel, eps=RMS_EPS),
        out_shape=jax.ShapeDtypeStruct((rows, d), out_dtype),
        grid=(rows // bm,),
        in_specs=[pl.BlockSpec((bm, d), lambda i: (i + off, 0)),
                  pl.BlockSpec((1, d), lambda i: (0, 0))],
        out_specs=pl.BlockSpec((bm, d), lambda i: (i, 0)),
        compiler_params=_params(("parallel",), 6 * bm * d * 4),
        name="rmsnorm",
    )(x, g.reshape(1, d).astype(F32))


def _matmul_kernel(*refs, nk, na, has_res, transpose_out):
    a_refs, w_ref = refs[:na], refs[na]
    r_ref = refs[na + 1] if has_res else None
    o_ref = refs[-1]

    def part():
        acc, start = None, 0
        for a_ref in a_refs:
            stop = start + a_ref.shape[1]
            d = jnp.dot(a_ref[...], w_ref[start:stop, :], preferred_element_type=F32)
            acc = d if acc is None else acc + d
            start = stop
        return acc

    if nk == 1:
        res = part() + r_ref[...] if r_ref is not None else part()
        o_ref[...] = (res.T if transpose_out else res).astype(o_ref.dtype)
        return
    k = pl.program_id(2)

    @pl.when(k == 0)
    def _():
        o_ref[...] = part() + r_ref[...] if r_ref is not None else part()

    @pl.when(k > 0)
    def _():
        o_ref[...] += part()


def matmul(a, w, out_dtype, *, residual=None, bm=1024, bn=1024, tk=None, transpose_out=False):
    a_list = list(a) if isinstance(a, (list, tuple)) else [a]
    m = a_list[0].shape[0]
    kdim, n = w.shape
    assert sum(x.shape[1] for x in a_list) == kdim
    tk = kdim if tk is None else tk
    bm, bn = min(bm, m), min(bn, n)
    assert m % bm == 0 and n % bn == 0 and kdim % tk == 0
    nk = kdim // tk
    assert nk == 1 or (len(a_list) == 1 and not transpose_out)
    has_res = residual is not None
    in_specs = [pl.BlockSpec((bm, x.shape[1] // nk), lambda i, j, k: (i, k)) for x in a_list]
    in_specs.append(pl.BlockSpec((tk, bn), lambda i, j, k: (k, j)))
    args = a_list + [w]
    if has_res:
        in_specs.append(pl.BlockSpec((bm, bn), lambda i, j, k: (i, j)))
        args.append(residual)
    assert nk == 1 or out_dtype == F32
    out_bytes = jnp.dtype(out_dtype).itemsize
    vmem = (2 * (bm * tk + tk * bn) * 2 + 2 * bm * bn * out_bytes + (2 * bm * bn * 4 if has_res else 0)
            + bm * bn * 4)
    return pl.pallas_call(
        functools.partial(_matmul_kernel, nk=nk, na=len(a_list), has_res=has_res, transpose_out=transpose_out),
        out_shape=jax.ShapeDtypeStruct((n, m) if transpose_out else (m, n), out_dtype),
        grid=(m // bm, n // bn, nk),
        in_specs=in_specs,
        out_specs=(pl.BlockSpec((bn, bm), lambda i, j, k: (j, i)) if transpose_out
                   else pl.BlockSpec((bm, bn), lambda i, j, k: (i, j))),
        compiler_params=_params(("parallel", "parallel", "arbitrary"), vmem),
        name="matmul",
    )(*args)


def _gate_up_kernel(a_ref, wg_ref, wu_ref, o_ref):
    a = a_ref[...]
    g = jnp.dot(a, wg_ref[...], preferred_element_type=F32)
    u = jnp.dot(a, wu_ref[...], preferred_element_type=F32)
    o_ref[...] = (g * jax.nn.sigmoid(g) * u).astype(o_ref.dtype)


def gate_up(h, wg, wu, *, bm=1024, bn=512):
    m, kdim = h.shape
    _, n = wg.shape
    assert m % bm == 0 and n % bn == 0
    vmem = 2 * bm * kdim * 2 + 4 * kdim * bn * 2 + 2 * bm * bn * 2 + 4 * bm * bn * 4
    return pl.pallas_call(
        _gate_up_kernel,
        out_shape=jax.ShapeDtypeStruct((m, n), BF16),
        grid=(m // bm, n // bn),
        in_specs=[pl.BlockSpec((bm, kdim), lambda i, j: (i, 0)),
                  pl.BlockSpec((kdim, bn), lambda i, j: (0, j)),
                  pl.BlockSpec((kdim, bn), lambda i, j: (0, j))],
        out_specs=pl.BlockSpec((bm, bn), lambda i, j: (i, j)),
        compiler_params=_params(("parallel", "parallel"), vmem),
        name="gate_up",
    )(h, wg, wu)


def _na_kernel(q_ref, k_ref, v_ref, bias_ref, o_ref, *, rows, kh):
    scale = HEAD_DIM ** -0.5
    win = kh * GRID_W

    def row_group(g, carry):
        rws = [g * NA_UNROLL + t for t in range(NA_UNROLL)]
        starts = [jnp.clip(r - kh // 2, 0, rows - kh) for r in rws]
        qoffs = [pl.multiple_of(r * GRID_W, GRID_W) for r in rws]
        koffs = [pl.multiple_of(rs * GRID_W, GRID_W) for rs in starts]
        ss = [lax.dot_general(q_ref[pl.ds(qo, GRID_W), :], k_ref[pl.ds(ko, win), :], (((1,), (1,)), ((), ())),
                              preferred_element_type=F32) for qo, ko in zip(qoffs, koffs)]
        ps, ls = [], []
        for s, r, rs in zip(ss, rws, starts):
            s = s * scale + bias_ref[r - rs]
            p = jnp.exp(s - jnp.max(s, axis=-1, keepdims=True))
            ls.append(jnp.sum(p, axis=-1, keepdims=True))
            ps.append(p.astype(v_ref.dtype))
        os_ = [jnp.dot(p, v_ref[pl.ds(ko, win), :], preferred_element_type=F32) for p, ko in zip(ps, koffs)]
        for o, l, qo in zip(os_, ls, qoffs):
            o_ref[pl.ds(qo, GRID_W), :] = (o / l).astype(o_ref.dtype)
        return carry

    assert rows % NA_UNROLL == 0
    lax.fori_loop(0, rows // NA_UNROLL, row_group, 0)


def _na_bias_table(rpb, kh):
    heads = rpb.shape[0]
    c = jnp.arange(GRID_W, dtype=jnp.int32)
    cs = jnp.clip(c - NA_COLS // 2, 0, GRID_W - NA_COLS)
    j = jnp.arange(GRID_W, dtype=jnp.int32)
    valid = (j[None, :] >= cs[:, None]) & (j[None, :] < cs[:, None] + NA_COLS)
    pad = GRID_W - NA_COLS
    w = jnp.pad(rpb.astype(F32), ((0, 0), (0, 0), (pad, pad)))
    t = jnp.where(valid, _toeplitz(w, GRID_W, GRID_W), NEG_BIG)
    t = jnp.transpose(t, (0, 2, 1, 3))
    tabs = [t[:, :, NA_ROWS - 1 - dl:NA_ROWS - 1 - dl + kh, :].reshape(heads, GRID_W, kh * GRID_W)
            for dl in range(kh)]
    return jnp.stack(tabs, axis=1)


def neighbourhood_attention(qkv, bias_tab, *, row_start, batch, seq, heads):
    rows = seq // GRID_W
    kh = bias_tab.shape[1]
    off = row_start // seq
    assert row_start % seq == 0
    vmem = 2 * 4 * seq * HEAD_DIM * 2 + 2 * bias_tab[0].size * 4 + (8 << 20)
    return pl.pallas_call(
        functools.partial(_na_kernel, rows=rows, kh=kh),
        out_shape=jax.ShapeDtypeStruct((batch * seq, heads * HEAD_DIM), BF16),
        grid=(batch, heads),
        in_specs=[pl.BlockSpec((seq, HEAD_DIM), lambda b, h: (b + off, h)),
                  pl.BlockSpec((seq, HEAD_DIM), lambda b, h: (b + off, heads + h)),
                  pl.BlockSpec((seq, HEAD_DIM), lambda b, h: (b + off, 2 * heads + h)),
                  pl.BlockSpec((None, kh, GRID_W, kh * GRID_W), lambda b, h: (h, 0, 0, 0))],
        out_specs=pl.BlockSpec((seq, HEAD_DIM), lambda b, h: (b, h)),
        compiler_params=_params(("parallel", "parallel"), vmem),
        name="neighbourhood_attention",
    )(qkv, qkv, qkv, bias_tab)


DFT_N2 = 64
DFT_CB = 8
LANES = 128


def _cos_sin(num, den):
    ang = (num % den).astype(F32) * (2.0 * math.pi / den)
    return jnp.cos(ang), jnp.sin(ang)


def _dft_stage1_kernel(w_ref, x_ref, o_ref):
    o_ref[...] = jnp.dot(w_ref[...], x_ref[...], preferred_element_type=F32).astype(o_ref.dtype)


def _dft_stage2_kernel(tr_ref, ti_ref, tc_ref, ts_ref, w2_ref, wg_ref, o_ref, *, groups, gdim):
    cb, n2, width = tr_ref.shape
    rep = width // LANES
    xr, xi = [], []
    for c in range(cb):
        tr = tr_ref[c].astype(F32)
        ti = ti_ref[c].astype(F32)
        tc = jnp.tile(tc_ref[c], (1, rep))
        ts = jnp.tile(ts_ref[c], (1, rep))
        stacked = jnp.concatenate([tr * tc + ti * ts, ti * tc - tr * ts], axis=0).astype(BF16)
        x = jnp.dot(w2_ref[...], stacked, preferred_element_type=F32)
        xr.append(x[:n2])
        xi.append(x[n2:])
    xr = jnp.concatenate(xr, axis=0).astype(BF16)
    xi = jnp.concatenate(xi, axis=0).astype(BF16)
    for g in range(groups):
        cols = slice(g * gdim, (g + 1) * gdim)
        f = (jnp.dot(xr[:, cols], wg_ref[0], preferred_element_type=F32)
             + jnp.dot(xi[:, cols], wg_ref[1], preferred_element_type=F32))
        for c in range(cb):
            o_ref[:, c * width + g * gdim:c * width + (g + 1) * gdim] = f[c * n2:(c + 1) * n2].astype(o_ref.dtype)


def fourier_mix(u, *, row_start, batch, seq, group_dim):
    total, width = u.shape
    groups = width // group_dim
    n2 = DFT_N2
    n1 = seq // n2
    cb = min(DFT_CB, n1)
    assert seq % n2 == 0 and n1 % cb == 0 and row_start % seq == 0 and total % n2 == 0
    row = n2 * width
    bn = min(row, 8192)
    a = jnp.arange(n1, dtype=jnp.int32)
    b = jnp.arange(n2, dtype=jnp.int32)
    c1, s1 = _cos_sin(a[:, None] * a[None, :], n1)
    w1 = (jnp.concatenate([c1, -s1], axis=0) * n1 ** -0.5).astype(BF16)
    off = row_start // seq
    t = pl.pallas_call(
        _dft_stage1_kernel,
        out_shape=jax.ShapeDtypeStruct((batch, 2 * n1, row), BF16),
        grid=(batch, row // bn),
        in_specs=[pl.BlockSpec((2 * n1, n1), lambda bi, j: (0, 0)),
                  pl.BlockSpec((n1, bn), lambda bi, j: (off + bi, j))],
        out_specs=pl.BlockSpec((None, 2 * n1, bn), lambda bi, j: (bi, 0, j)),
        compiler_params=_params(("parallel", "parallel"), 2 * (3 * n1 * bn * 2 + 2 * n1 * n1 * 2) + 2 * n1 * bn * 4),
        name="dft_stage1",
    )(w1, u.reshape(total // n2, row))
    t = t.reshape(batch, 2 * n1, n2, width)
    tc, ts = _cos_sin(a[:, None] * b[None, :], seq)
    tc = jnp.broadcast_to(tc[:, :, None], (n1, n2, LANES))
    ts = jnp.broadcast_to(ts[:, :, None], (n1, n2, LANES))
    c2, s2 = _cos_sin(b[:, None] * b[None, :], n2)
    w2 = (jnp.block([[c2, s2], [-s2, c2]]) * n2 ** -0.5).astype(BF16)
    gi = jnp.arange(group_dim, dtype=jnp.int32)
    cg, sg = _cos_sin(gi[:, None] * gi[None, :], group_dim)
    wg = (jnp.stack([cg, sg]) * group_dim ** -0.5).astype(BF16)
    nj = n1 // cb
    vmem = (2 * (2 * cb * n2 * width * 2 + 2 * cb * n2 * LANES * 4 + n2 * cb * width * 2) + 4 * n2 * n2 * 2
            + 2 * wg.size * 2 + 6 * cb * n2 * width * 4)
    out = pl.pallas_call(
        functools.partial(_dft_stage2_kernel, groups=groups, gdim=group_dim),
        out_shape=jax.ShapeDtypeStruct((batch, n2, n1 * width), BF16),
        grid=(batch, nj),
        in_specs=[pl.BlockSpec((None, cb, n2, width), lambda bi, j: (bi, j, 0, 0)),
                  pl.BlockSpec((None, cb, n2, width), lambda bi, j: (bi, nj + j, 0, 0)),
                  pl.BlockSpec((cb, n2, LANES), lambda bi, j: (j, 0, 0)),
                  pl.BlockSpec((cb, n2, LANES), lambda bi, j: (j, 0, 0)),
                  pl.BlockSpec((2 * n2, 2 * n2), lambda bi, j: (0, 0)),
                  pl.BlockSpec((2, group_dim, group_dim), lambda bi, j: (0, 0, 0))],
        out_specs=pl.BlockSpec((None, n2, cb * width), lambda bi, j: (bi, 0, j)),
        compiler_params=_params(("parallel", "parallel"), vmem),
        name="dft_stage2",
    )(t, t, tc, ts, w2, wg)
    return out.reshape(batch * seq, width)


def _t5_bucket(rel):
    nb = T5_BUCKETS // 2
    max_exact = nb // 2
    ret = (rel > 0).astype(jnp.int32) * nb
    n = jnp.abs(rel)
    nf = jnp.maximum(n, 1).astype(F32)
    large = max_exact + (jnp.log(nf / max_exact) / math.log(T5_MAX_DIST / max_exact)
                         * (nb - max_exact)).astype(jnp.int32)
    large = jnp.minimum(large, nb - 1)
    return ret + jnp.where(n < max_exact, n, large)


ATT_BIAS_CHUNKS = (-2, -1, 0, 1, 2)


def _t5_bias_tiles(t5_bias):
    assert ATT_C == ATT_R and ATT_R + 1 >= T5_MAX_DIST
    tiles = []
    for dc in ATT_BIAS_CHUNKS:
        rel = dc * ATT_C + ATT_C - 1 - jnp.arange(ATT_C + ATT_R - 1, dtype=jnp.int32)
        w = t5_bias.astype(F32)[_t5_bucket(rel)].T * LOG2E
        tiles.append(_toeplitz(w, ATT_C, ATT_R))
    return jnp.stack(tiles, axis=1)


ATT_SUM_ROWS = 16

def _diff_attn_kernel(qt_ref, k_ref, vt_ref, nb_ref, lq1_ref, lk1_ref, lq2_ref, lk2_ref, w_ref, o_ref,
                      m_ref, a_ref, s_ref, p_ref, al_ref, *, nc, nsub, lam_init):
    i = pl.program_id(2)
    d = HEAD_DIM
    m_ref[...] = jnp.full(m_ref.shape, NEG_BIG, F32)
    a_ref[...] = jnp.zeros(a_ref.shape, F32)
    lo, hi = ATT_BIAS_CHUNKS[0], ATT_BIAS_CHUNKS[-1]
    units = [(r, half) for r in range(nsub) for half in (0, 1)]

    def scores(c, slot):
        for r, half in units:
            kc = k_ref[pl.ds(pl.multiple_of(c * ATT_C, ATT_C), ATT_C), half * d:(half + 1) * d]
            bias = nb_ref[jnp.clip(c - (i * nsub + r), lo, hi) - lo]
            qt = qt_ref[half * d:(half + 1) * d, r * ATT_R:(r + 1) * ATT_R]
            s_ref[slot, r, half] = jnp.dot(kc, qt, preferred_element_type=F32) + bias

    def softmax(slot):
        for r, half in units:
            s = s_ref[slot, r, half]
            m_old = m_ref[r, half]
            m_new = jnp.maximum(m_old, jnp.max(s, axis=0, keepdims=True))
            m_ref[r, half] = m_new
            al_ref[slot, r, half] = jnp.exp2(m_old - m_new)
            p_ref[slot, r, half] = jnp.exp2(s - m_new).astype(p_ref.dtype)

    def accumulate(c, slot):
        for r, half in units:
            a_ref[r, half] = (al_ref[slot, r, half] * a_ref[r, half]
                              + jnp.dot(vt_ref[c], p_ref[slot, r, half], preferred_element_type=F32))

    def advance(c, t):
        accumulate(c - 2, t)
        softmax(1 - t)
        scores(c, t)

    scores(0, 0)
    scores(1, 1)
    softmax(0)
    for c in range(2, ATT_UNROLL):
        advance(c, c % 2)

    def step(j, carry):
        for t in range(ATT_UNROLL):
            advance(ATT_UNROLL * j + t, t % 2)
        return carry

    lax.fori_loop(1, nc // ATT_UNROLL, step, 0)
    accumulate(nc - 2, 0)
    softmax(1)
    accumulate(nc - 1, 1)

    lam = (jnp.exp(jnp.sum(lq1_ref[...] * lk1_ref[...], axis=-1, keepdims=True))
           - jnp.exp(jnp.sum(lq2_ref[...] * lk2_ref[...], axis=-1, keepdims=True)) + lam_init)
    hd2 = 2 * d
    for r in range(nsub):
        o1, o2 = (a_ref[r, half, :hd2] / a_ref[r, half, hd2:hd2 + 1] for half in (0, 1))
        att = o1 - lam * o2
        inv = lax.rsqrt(jnp.mean(att * att, axis=0, keepdims=True) + SUBLN_EPS)
        y = (att * inv).T * w_ref[...] * (1.0 - lam_init)
        o_ref[r * ATT_R:(r + 1) * ATT_R, :] = y.astype(o_ref.dtype)


def differential_attention(qt, proj, vt, bias_tiles, lam_vecs, subln_w, *, row_start, batch, seq, heads, lam_init):
    tq = min(ATT_TQ, seq)
    nq = seq // tq
    nc = seq // ATT_C
    nsub = tq // ATT_R
    assert seq % tq == 0 and row_start % seq == 0 and tq % ATT_R == 0 and nc % ATT_UNROLL == 0
    qoff = row_start // tq
    soff = row_start // seq
    hd2 = 2 * HEAD_DIM
    rows_a = hd2 + ATT_SUM_ROWS
    vec = pl.BlockSpec((1, HEAD_DIM), lambda b, h, i: (0, 0))
    vmem = (2 * seq * (hd2 + rows_a) * 2 + 2 * bias_tiles[0].size * 4 + 4 * tq * hd2 * 2
            + 2 * nsub * ATT_R * ((rows_a + 3 * 8) * 4 + 2 * ATT_C * 6))
    return pl.pallas_call(
        functools.partial(_diff_attn_kernel, nc=nc, nsub=nsub, lam_init=lam_init),
        out_shape=jax.ShapeDtypeStruct((batch * seq, heads * hd2), BF16),
        grid=(batch, heads, nq),
        in_specs=[pl.BlockSpec((None, hd2, tq), lambda b, h, i: (h, 0, qoff + b * nq + i)),
                  pl.BlockSpec((seq, hd2), lambda b, h, i: (soff + b, h)),
                  pl.BlockSpec((None, nc, rows_a, ATT_C), lambda b, h, i: (h, soff + b, 0, 0)),
                  pl.BlockSpec((None,) + bias_tiles.shape[1:], lambda b, h, i: (h, 0, 0, 0)),
                  vec, vec, vec, vec,
                  pl.BlockSpec((1, hd2), lambda b, h, i: (0, 0))],
        out_specs=pl.BlockSpec((tq, hd2), lambda b, h, i: (b * nq + i, h)),
        scratch_shapes=[pltpu.VMEM((nsub, 2, 1, ATT_R), F32), pltpu.VMEM((nsub, 2, rows_a, ATT_R), F32),
                        pltpu.VMEM((2, nsub, 2, ATT_C, ATT_R), F32), pltpu.VMEM((2, nsub, 2, ATT_C, ATT_R), BF16),
                        pltpu.VMEM((2, nsub, 2, 1, ATT_R), F32)],
        compiler_params=_params(("parallel", "parallel", "arbitrary"), vmem),
        name="differential_attention",
    )(qt, proj, vt, bias_tiles, *lam_vecs, subln_w.reshape(1, hd2).astype(F32))


def _cast_pad_kernel(x_ref, o_ref, *, row_blocks, pad_rows):
    cols = x_ref.shape[1]

    def copy():
        o_ref[:, :cols] = x_ref[...].astype(o_ref.dtype)
        if cols < o_ref.shape[1]:
            o_ref[:, cols:] = jnp.zeros((o_ref.shape[0], o_ref.shape[1] - cols), o_ref.dtype)

    if not pad_rows:
        copy()
    else:
        pl.when(pl.program_id(0) < row_blocks)(copy)

        @pl.when(pl.program_id(0) >= row_blocks)
        def _():
            o_ref[...] = jnp.zeros(o_ref.shape, o_ref.dtype)


def cast_pad(w, layer, rows_to, cols_to, *, bm=LANES):
    _, rows, cols = w.shape
    assert rows % bm == 0 and rows_to % bm == 0 and cols % LANES == 0 and cols_to % LANES == 0
    row_blocks = rows // bm
    return pl.pallas_call(
        functools.partial(_cast_pad_kernel, row_blocks=row_blocks, pad_rows=rows_to > rows),
        out_shape=jax.ShapeDtypeStruct((rows_to, cols_to), BF16),
        grid=(rows_to // bm,),
        in_specs=[pl.BlockSpec((None, bm, cols), lambda i: (layer, jnp.minimum(i, row_blocks - 1), 0))],
        out_specs=pl.BlockSpec((bm, cols_to), lambda i: (i, 0)),
        compiler_params=_params(("parallel",), 2 * bm * (cols * 4 + cols_to * 2)),
        name="cast_pad",
    )(w)


def _round_up(n, mult):
    return -(-n // mult) * mult


def kernel(x_prompt, x_sample, norm_mix, norm_ffn, w_in_even, rpb_na, w_out_even, w_in_odd,
           lambda_q1, lambda_k1, lambda_q2, lambda_k2, subln_w, w_out_odd, t5_bias,
           w_gate, w_up, w_down, norm_final):
    d_model = x_prompt.shape[-1]
    depth = norm_mix.shape[0]
    na_heads = rpb_na.shape[1]
    na_width = na_heads * HEAD_DIM
    diff_heads = t5_bias.shape[1]
    diff_width = diff_heads * 2 * HEAD_DIM
    fnet_width = w_in_even.shape[-1] - 3 * na_width
    group_dim = fnet_width // FNET_GROUPS
    hidden = _round_up(w_gate.shape[-1], FFN_PAD)

    layers = []
    for layer in range(depth):
        lw = {}
        if layer % 2 == 0:
            e = layer // 2
            w_in = w_in_even[e].astype(BF16)
            lw.update(w_qkv=w_in[:, :3 * na_width], w_u=w_in[:, 3 * na_width:], w_out=w_out_even[e].astype(BF16),
                      rpb=rpb_na[e])
        else:
            o = layer // 2
            col_scale = jnp.where(jnp.arange(3 * diff_width) < diff_width, HEAD_DIM ** -0.5 * LOG2E, 1.0)
            w_in = (w_in_odd[o] * col_scale.astype(F32)).astype(BF16)
            lw.update(w_q=w_in[:, :diff_width], w_kv=w_in[:, diff_width:], w_out=w_out_odd[o].astype(BF16),
                      bias_tiles=_t5_bias_tiles(t5_bias), subln_w=subln_w[o],
                      lam_vecs=[v[o].reshape(1, HEAD_DIM).astype(F32)
                                for v in (lambda_q1, lambda_k1, lambda_q2, lambda_k2)])
        lw.update(wg=cast_pad(w_gate, layer, d_model, hidden), wu=cast_pad(w_up, layer, d_model, hidden),
                  wd=cast_pad(w_down, layer, hidden, d_model))
        layers.append(lw)

    return tuple(_trunk(xb, layers, norm_mix, norm_ffn, norm_final, na_heads=na_heads, diff_heads=diff_heads,
                        group_dim=group_dim) for xb in (x_prompt, x_sample))


def _trunk(xb, layers, norm_mix, norm_ffn, norm_final, *, na_heads, diff_heads, group_dim):
    batch, seq, d_model = xb.shape
    diff_width = diff_heads * 2 * HEAD_DIM
    x = xb.reshape(batch * seq, d_model)
    for layer, lw in enumerate(layers):
        h = rmsnorm(x, norm_mix[layer], BF16)
        if layer % 2 == 0:
            qkv = matmul(h, lw["w_qkv"], BF16)
            u = matmul(h, lw["w_u"], BF16)
            kh = min(NA_ROWS, seq // GRID_W)
            oa = neighbourhood_attention(qkv, _na_bias_table(lw["rpb"], kh), row_start=0, batch=batch, seq=seq,
                                         heads=na_heads)
            ob = fourier_mix(u, row_start=0, batch=batch, seq=seq, group_dim=group_dim)
            x = matmul([oa, ob], lw["w_out"], F32, residual=x)
        else:
            qt = matmul(h, lw["w_q"], BF16, transpose_out=True).reshape(diff_heads, 2 * HEAD_DIM, -1)
            proj = matmul(h, lw["w_kv"], BF16)
            vt = proj[:, diff_width:].reshape(-1, ATT_C, diff_heads, 2 * HEAD_DIM)
            vt = jnp.transpose(vt, (2, 0, 3, 1))
            vt = jnp.concatenate([vt, jnp.ones(vt.shape[:2] + (ATT_SUM_ROWS, ATT_C), BF16)], axis=2)
            lam_init = 0.8 - 0.6 * math.exp(-0.3 * layer)
            att = differential_attention(qt, proj, vt, lw["bias_tiles"], lw["lam_vecs"], lw["subln_w"], row_start=0,
                                         batch=batch, seq=seq, heads=diff_heads, lam_init=lam_init)
            x = matmul(att, lw["w_out"], F32, residual=x)
        h = rmsnorm(x, norm_ffn[layer], BF16)
        act = gate_up(h, lw["wg"], lw["wu"])
        x = matmul(act, lw["wd"], F32, residual=x, bn=2048, tk=1024)
    return rmsnorm(x, norm_final, F32).reshape(xb.shape)
```
